```python
import jax, jax.numpy as jnp
from jax import lax
import numpy as np

D_MODEL = 4096
BATCH = 2
SEQ = 4096
DEPTH = 2

HEAD_DIM = 128
NSA_HEADS = (D_MODEL // 2) // HEAD_DIM
NSA_KV_HEADS = 4
NSA_HPG = NSA_HEADS // NSA_KV_HEADS
NSA_WIDTH = NSA_HEADS * HEAD_DIM
KV_WIDTH = NSA_KV_HEADS * HEAD_DIM
CMP_STRIDE = 16
CMP_BLOCK = 2 * CMP_STRIDE
CMP_HIDDEN = 256
SLC_BLOCK = 64
SLC_TOPK = 16
WINDOW = 512
N_GATES = 3 * NSA_HEADS
ATTN_SCALE = HEAD_DIM ** -0.5
ROPE_THETA = 500000.0
ROPE_DIM = HEAD_DIM // 4
POOL_WIDTH = D_MODEL - NSA_WIDTH
POOL_WINDOWS = (2, 4, 8, 16)
POOL_GROUP = POOL_WIDTH // len(POOL_WINDOWS)
IN_EVEN = NSA_WIDTH + 6 * KV_WIDTH + N_GATES + POOL_WIDTH
OUT_EVEN = NSA_WIDTH + POOL_WIDTH
GMLP_CHUNK = 128
GMLP_WIDTH = 3 * D_MODEL
GMLP_GROUPS = 16
GMLP_GROUP_DIM = GMLP_WIDTH // GMLP_GROUPS
FFN_HIDDEN = -(-8 * D_MODEL // (3 * 256)) * 256
WIN_Q_BLOCK = 128
SLC_Q_BLOCK = 64
N_EVEN = (DEPTH + 1) // 2
N_ODD = DEPTH // 2
EPS = 1e-6
NEG = -1e30

kernel_name = 'hybrid_nsa_pool_gmlp_trunk'


def rms_norm(x, g):
    xf = x.astype(jnp.float32)
    y = xf * lax.rsqrt(jnp.mean(xf * xf, axis=-1, keepdims=True) + EPS)
    return (y * g.astype(jnp.float32)).astype(x.dtype)


def partial_rope(x, pos):
    half = ROPE_DIM // 2
    inv_freq = ROPE_THETA ** (-jnp.arange(half, dtype=jnp.float32) * 2.0 / ROPE_DIM)
    ang = pos.astype(jnp.float32)[:, None] * inv_freq[None, :]
    cos = jnp.cos(ang)[None, :, None, :]
    sin = jnp.sin(ang)[None, :, None, :]
    xf = x.astype(jnp.float32)
    x1 = xf[..., :half]
    x2 = xf[..., half:ROPE_DIM]
    out = jnp.concatenate([x1 * cos - x2 * sin, x2 * cos + x1 * sin, xf[..., ROPE_DIM:]], axis=-1)
    return out.astype(x.dtype)


def compress(k, pe, w1, w2):
    B, T, G, D = k.shape
    chunks = k.reshape(B, T // CMP_STRIDE, CMP_STRIDE, G, D)
    blocks = jnp.concatenate([chunks[:, :-1], chunks[:, 1:]], axis=2)
    blocks = blocks + pe[None, None, :, None, :]
    hid = jax.nn.gelu(jnp.einsum('bnlgd,ldh->bngh', blocks, w1))
    return jnp.einsum('bngh,hd->bngd', hid, w2)


def cmp_attention(q, kc, vc, pos):
    NC = kc.shape[1]
    s = jnp.einsum('btghd,bngd->bghtn', q, kc).astype(jnp.float32) * ATTN_SCALE
    blk_end = jnp.arange(NC) * CMP_STRIDE + CMP_BLOCK - 1
    valid = blk_end[None, :] <= pos[:, None]
    s = jnp.where(valid, s, NEG)
    p = jnp.where(valid, jax.nn.softmax(s, axis=-1), 0.0)
    o = jnp.einsum('bghtn,bngd->btghd', p.astype(vc.dtype), vc)
    return o, p


def select_blocks(p_cmp, pos, n_slc):
    imp = p_cmp.sum(axis=2)
    B, G, T, _ = imp.shape
    r = SLC_BLOCK // CMP_STRIDE
    padded = jnp.pad(imp, ((0, 0), (0, 0), (0, 0), (1, 1)))
    main = padded[..., : r * n_slc].reshape(B, G, T, n_slc, r)
    nxt = padded[..., r: r * n_slc + 1: r]
    score = main.sum(-1) - 0.5 * main[..., 0] + 0.5 * nxt
    blk = jnp.arange(n_slc)[None, :]
    cur = (pos // SLC_BLOCK)[:, None]
    forced = (blk == cur) | (blk == 0)
    score = jnp.where(forced, -NEG, jnp.where(blk > cur, NEG, score))
    _, idx = lax.top_k(score, min(SLC_TOPK, n_slc))
    return idx


def slc_attention(q, ks, vs, idx, pos):
    B, T, G, HPG, D = q.shape
    n_slc = T // SLC_BLOCK
    kb = ks.reshape(B, n_slc, SLC_BLOCK, G, D).transpose(0, 3, 1, 2, 4)
    vb = vs.reshape(B, n_slc, SLC_BLOCK, G, D).transpose(0, 3, 1, 2, 4)
    nq = T // SLC_Q_BLOCK
    k = idx.shape[-1]
    q_blocks = q.reshape(B, nq, SLC_Q_BLOCK, G, HPG, D).transpose(1, 0, 2, 3, 4, 5)
    i_blocks = idx.reshape(B, G, nq, SLC_Q_BLOCK, k).transpose(2, 0, 1, 3, 4)
    p_blocks = pos.reshape(nq, SLC_Q_BLOCK)
    gather = jax.vmap(jax.vmap(lambda blocks, ids: blocks[ids]))

    def one(args):
        qc, ic, pc = args
        kg = gather(kb, ic)
        vg = gather(vb, ic)
        s = jnp.einsum('bqghd,bgqkld->bghqkl', qc, kg).astype(jnp.float32) * ATTN_SCALE
        kpos = ic[..., None] * SLC_BLOCK + jnp.arange(SLC_BLOCK)
        valid = kpos <= pc[None, None, :, None, None]
        s = jnp.where(valid[:, :, None], s, NEG)
        s = s.reshape(B, G, HPG, SLC_Q_BLOCK, k * SLC_BLOCK)
        p = jax.nn.softmax(s, axis=-1).reshape(B, G, HPG, SLC_Q_BLOCK, k, SLC_BLOCK)
        return jnp.einsum('bghqkl,bgqkld->bqghd', p.astype(vg.dtype), vg)

    o = lax.map(one, (q_blocks, i_blocks, p_blocks))
    return o.transpose(1, 0, 2, 3, 4, 5).reshape(B, T, G, HPG, D)


def win_attention(q, kw, vw):
    B, T, G, HPG, D = q.shape
    kp = jnp.pad(kw, ((0, 0), (WINDOW, 0), (0, 0), (0, 0)))
    vp = jnp.pad(vw, ((0, 0), (WINDOW, 0), (0, 0), (0, 0)))
    nq = T // WIN_Q_BLOCK
    span = WINDOW + WIN_Q_BLOCK
    q_blocks = q.reshape(B, nq, WIN_Q_BLOCK, G, HPG, D).transpose(1, 0, 2, 3, 4, 5)

    def one(args):
        qc, i = args
        s0 = i * WIN_Q_BLOCK
        kc = lax.dynamic_slice_in_dim(kp, s0, span, axis=1)
        vc = lax.dynamic_slice_in_dim(vp, s0, span, axis=1)
        qpos = s0 + jnp.arange(WIN_Q_BLOCK)
        kpos = s0 - WINDOW + jnp.arange(span)
        valid = ((kpos[None, :] <= qpos[:, None]) & (kpos[None, :] > qpos[:, None] - WINDOW)
                 & (kpos[None, :] >= 0))
        s = jnp.einsum('bqghd,bkgd->bghqk', qc, kc).astype(jnp.float32) * ATTN_SCALE
        s = jnp.where(valid, s, NEG)
        p = jax.nn.softmax(s, axis=-1)
        return jnp.einsum('bghqk,bkgd->bqghd', p.astype(vc.dtype), vc)

    o = lax.map(one, (q_blocks, jnp.arange(nq)))
    return o.transpose(1, 0, 2, 3, 4, 5).reshape(B, T, G, HPG, D)


def pool_mixer(p, w, scale):
    B, T, C = p.shape
    xg = p.reshape(B, T, len(POOL_WINDOWS), POOL_GROUP).astype(jnp.float32)
    c = jnp.pad(jnp.cumsum(xg, axis=1), ((0, 0), (1, 0), (0, 0), (0, 0)))
    outs = []
    for gi, w_len in enumerate(POOL_WINDOWS):
        cg = c[:, :, gi]
        lag = jnp.pad(cg, ((0, 0), (w_len, 0), (0, 0)))[:, 1:T + 1]
        count = jnp.minimum(jnp.arange(T) + 1, w_len).astype(jnp.float32)
        outs.append((cg[:, 1:] - lag) / count[None, :, None] - xg[:, :, gi])
    y = jnp.stack(outs, axis=2).astype(p.dtype)
    y = jnp.einsum('btgc,gcd->btgd', y, w).reshape(B, T, C)
    return y * scale


def even_mixer(h, pos, w_in, pe_k, pe_v, w1_k, w2_k, w1_v, w2_v, pool_w, pool_scale, w_out):
    B, T, _ = h.shape
    z = h @ w_in
    splits = np.cumsum([NSA_WIDTH] + [KV_WIDTH] * 6 + [N_GATES]).tolist()
    q, kc, vc, ks, vs, kw, vw, gates, pin = jnp.split(z, splits, axis=-1)
    kvshape = (B, T, NSA_KV_HEADS, HEAD_DIM)
    q = partial_rope(q.reshape(B, T, NSA_HEADS, HEAD_DIM), pos)
    q = q.reshape(B, T, NSA_KV_HEADS, NSA_HPG, HEAD_DIM)
    kc = partial_rope(kc.reshape(kvshape), pos)
    ks = partial_rope(ks.reshape(kvshape), pos)
    kw = partial_rope(kw.reshape(kvshape), pos)
    vc, vs, vw = vc.reshape(kvshape), vs.reshape(kvshape), vw.reshape(kvshape)
    k_cmp = compress(kc, pe_k, w1_k, w2_k)
    v_cmp = compress(vc, pe_v, w1_v, w2_v)
    o_cmp, p_cmp = cmp_attention(q, k_cmp, v_cmp, pos)
    idx = select_blocks(p_cmp, pos, T // SLC_BLOCK)
    o_slc = slc_attention(q, ks, vs, idx, pos)
    o_win = win_attention(q, kw, vw)
    g = jax.nn.sigmoid(gates.astype(jnp.float32)).astype(h.dtype)
    g = g.reshape(B, T, NSA_KV_HEADS, NSA_HPG, 3)
    o = g[..., 0:1] * o_cmp + g[..., 1:2] * o_slc + g[..., 2:3] * o_win
    o = o.reshape(B, T, NSA_WIDTH)
    y_pool = pool_mixer(pin, pool_w, pool_scale)
    return jnp.concatenate([o, y_pool], axis=-1) @ w_out


def odd_mixer(h, w_in, ln_g, ln_b, ws, bs, w_out):
    B, T, _ = h.shape
    z = jax.nn.gelu(h @ w_in)
    u, v = jnp.split(z, 2, axis=-1)
    vf = v.astype(jnp.float32)
    mu = jnp.mean(vf, axis=-1, keepdims=True)
    var = jnp.mean(jnp.square(vf - mu), axis=-1, keepdims=True)
    v = ((vf - mu) * lax.rsqrt(var + EPS) * ln_g + ln_b).astype(h.dtype)
    nc = T // GMLP_CHUNK
    v = v.reshape(B, nc, GMLP_CHUNK, GMLP_GROUPS, GMLP_GROUP_DIM)
    mask = jnp.tril(jnp.ones((GMLP_CHUNK, GMLP_CHUNK), dtype=ws.dtype))
    s = jnp.einsum('gts,bcsgd->bctgd', ws * mask, v) + bs.T[None, None, :, :, None]
    y = u * s.reshape(B, T, GMLP_WIDTH)
    return y @ w_out


def swiglu(h, wg, wu, wd):
    return (jax.nn.silu(h @ wg) * (h @ wu)) @ wd


def setup_inputs(seed: int = 0) -> dict:
    key = jax.random.key(seed)
    ks = jax.random.split(key, 24)
    f32 = jnp.float32

    def nrm(k, shape, fan_in):
        return jax.random.normal(k, shape, f32) * (fan_in ** -0.5)

    def gain(k, shape):
        return 1.0 + 0.02 * jax.random.normal(k, shape, f32)

    def small(k, shape):
        return 0.02 * jax.random.normal(k, shape, f32)

    return {
        'x': jax.random.normal(ks[0], (BATCH, SEQ, D_MODEL), f32),
        'norm_mix_even': gain(ks[1], (N_EVEN, D_MODEL)),
        'w_in_even': nrm(ks[2], (N_EVEN, D_MODEL, IN_EVEN), D_MODEL),
        'cmp_pe_k': small(ks[3], (N_EVEN, CMP_BLOCK, HEAD_DIM)),
        'cmp_pe_v': small(ks[4], (N_EVEN, CMP_BLOCK, HEAD_DIM)),
        'cmp_w1_k': nrm(ks[5], (N_EVEN, CMP_BLOCK, HEAD_DIM, CMP_HIDDEN), CMP_BLOCK * HEAD_DIM),
        'cmp_w2_k': nrm(ks[6], (N_EVEN, CMP_HIDDEN, HEAD_DIM), CMP_HIDDEN),
        'cmp_w1_v': nrm(ks[7], (N_EVEN, CMP_BLOCK, HEAD_DIM, CMP_HIDDEN), CMP_BLOCK * HEAD_DIM),
        'cmp_w2_v': nrm(ks[8], (N_EVEN, CMP_HIDDEN, HEAD_DIM), CMP_HIDDEN),
        'pool_w': nrm(ks[9], (N_EVEN, len(POOL_WINDOWS), POOL_GROUP, POOL_GROUP), POOL_GROUP),
        'pool_scale': gain(ks[10], (N_EVEN, POOL_WIDTH)),
        'w_out_even': nrm(ks[11], (N_EVEN, OUT_EVEN, D_MODEL), OUT_EVEN),
        'norm_mix_odd': gain(ks[12], (N_ODD, D_MODEL)),
        'w_in_odd': nrm(ks[13], (N_ODD, D_MODEL, 2 * GMLP_WIDTH), D_MODEL),
        'gmlp_ln_g': gain(ks[14], (N_ODD, GMLP_WIDTH)),
        'gmlp_ln_b': small(ks[15], (N_ODD, GMLP_WIDTH)),
        'gmlp_ws': nrm(ks[16], (N_ODD, GMLP_GROUPS, GMLP_CHUNK, GMLP_CHUNK), GMLP_CHUNK),
        'gmlp_bs': gain(ks[17], (N_ODD, GMLP_GROUPS, GMLP_CHUNK)),
        'w_out_odd': nrm(ks[18], (N_ODD, GMLP_WIDTH, D_MODEL), GMLP_WIDTH),
        'norm_ffn': gain(ks[19], (DEPTH, D_MODEL)),
        'w_ffn_gate': nrm(ks[20], (DEPTH, D_MODEL, FFN_HIDDEN), D_MODEL),
        'w_ffn_up': nrm(ks[21], (DEPTH, D_MODEL, FFN_HIDDEN), D_MODEL),
        'w_ffn_down': nrm(ks[22], (DEPTH, FFN_HIDDEN, D_MODEL), FFN_HIDDEN),
        'norm_final': gain(ks[23], (D_MODEL,)),
    }


def reference(x, norm_mix_even, w_in_even, cmp_pe_k, cmp_pe_v, cmp_w1_k, cmp_w2_k, cmp_w1_v,
              cmp_w2_v, pool_w, pool_scale, w_out_even, norm_mix_odd, w_in_odd, gmlp_ln_g,
              gmlp_ln_b, gmlp_ws, gmlp_bs, w_out_odd, norm_ffn, w_ffn_gate, w_ffn_up, w_ffn_down,
              norm_final):
    pos = jnp.arange(x.shape[1])
    for layer in range(DEPTH):
        j = layer // 2
        if layer % 2 == 0:
            h = rms_norm(x, norm_mix_even[j])
            x = x + even_mixer(h, pos, w_in_even[j], cmp_pe_k[j], cmp_pe_v[j], cmp_w1_k[j],
                               cmp_w2_k[j], cmp_w1_v[j], cmp_w2_v[j], pool_w[j], pool_scale[j],
                               w_out_even[j])
        else:
            h = rms_norm(x, norm_mix_odd[j])
            x = x + odd_mixer(h, w_in_odd[j], gmlp_ln_g[j], gmlp_ln_b[j], gmlp_ws[j], gmlp_bs[j],
                              w_out_odd[j])
        h = rms_norm(x, norm_ffn[layer])
        x = x + swiglu(h, w_ffn_gate[layer], w_ffn_up[layer], w_ffn_down[layer])
    return rms_norm(x, norm_final)
```

```python
import functools

import jax
import jax.numpy as jnp
from jax import lax
from jax.experimental import pallas as pl
from jax.experimental.pallas import tpu as pltpu

HEAD_DIM = 128
NSA_HEADS = 16
NSA_KV_HEADS = 4
NSA_HPG = NSA_HEADS // NSA_KV_HEADS
NSA_WIDTH = NSA_HEADS * HEAD_DIM
KV_WIDTH = NSA_KV_HEADS * HEAD_DIM
CMP_STRIDE = 16
CMP_BLOCK = 32
CMP_HIDDEN = 256
SLC_BLOCK = 64
SLC_TOPK = 16
WINDOW = 512
N_GATES = 3 * NSA_HEADS
ATTN_SCALE = HEAD_DIM ** -0.5
ROPE_THETA = 500000.0
ROPE_DIM = HEAD_DIM // 4
POOL_WINDOWS = (2, 4, 8, 16)
POOL_GROUP = 512
POOL_HALO = 16
GMLP_CHUNK = 128
GMLP_GROUPS = 16
EPS = 1e-6
NEG = -1e30

LANE = 128
VMEM_LIMIT_BYTES = 56 * 1024 * 1024
ATT_TQ = 128
ATT_CK = 256
FFN_PAD = 512
MM_VMEM_BUDGET_BYTES = 48 * 1024 * 1024

bf16 = jnp.bfloat16
f32 = jnp.float32


def _cparams(sem):
    return pltpu.CompilerParams(dimension_semantics=sem, vmem_limit_bytes=VMEM_LIMIT_BYTES)


def _rms_kernel(x_ref, g_ref, o_ref):
    x = x_ref[...]
    ms = jnp.mean(x * x, axis=-1, keepdims=True)
    o_ref[...] = (x * lax.rsqrt(ms + EPS) * g_ref[...]).astype(o_ref.dtype)


def rms_norm(x, g, out_dtype, tm=256):
    m, d = x.shape
    return pl.pallas_call(
        _rms_kernel,
        grid=(m // tm,),
        in_specs=[pl.BlockSpec((tm, d), lambda i: (i, 0)), pl.BlockSpec((1, d), lambda i: (0, 0))],
        out_specs=pl.BlockSpec((tm, d), lambda i: (i, 0)),
        out_shape=jax.ShapeDtypeStruct((m, d), out_dtype),
        compiler_params=_cparams(("parallel",)),
        name="rms_norm",
    )(x, g.reshape(1, d))


def _gelu_tanh(x):
    return 0.5 * x * (1.0 + jnp.tanh(0.7978845608028654 * (x + 0.044715 * (x * x * x))))


def _mm_kernel(*refs, nk, has_res, act):
    if has_res:
        a_ref, w_ref, r_ref, o_ref = refs[:4]
        scratch = refs[4:]
    else:
        a_ref, w_ref, o_ref = refs[:3]
        r_ref = None
        scratch = refs[3:]

    def finish(acc):
        if act == "gelu":
            acc = _gelu_tanh(acc)
        if has_res:
            acc = acc + r_ref[...]
        o_ref[...] = acc.astype(o_ref.dtype)

    d = jnp.dot(a_ref[...], w_ref[...], preferred_element_type=f32)
    if nk == 1:
        finish(d)
        return
    acc_ref = scratch[0]
    k = pl.program_id(2)

    @pl.when(k == 0)
    def _():
        acc_ref[...] = d

    @pl.when(jnp.logical_and(k > 0, k < nk - 1))
    def _():
        acc_ref[...] += d

    @pl.when(k == nk - 1)
    def _():
        finish(acc_ref[...] + d)


def _pick_bk(kdim, bm, bn, out_bytes, has_res):
    fixed = bm * bn * (4 + 2 * out_bytes + (8 if has_res else 0))
    for nk in range(1, kdim // LANE + 1):
        bk = kdim // nk
        acc = bm * bn * 4 if nk > 1 else 0
        if kdim % nk == 0 and bk % LANE == 0 and fixed + acc + 4 * bk * (bm + bn) <= MM_VMEM_BUDGET_BYTES:
            return bk
    raise ValueError("no K tile fits")


def matmul(a, w, res=None, act=None, out_dtype=f32, bm=1024, bn=1024):
    m, kdim = a.shape
    _, n = w.shape
    bn = min(bn, n)
    bk = _pick_bk(kdim, bm, bn, jnp.dtype(out_dtype).itemsize, res is not None)
    assert m % bm == 0 and n % bn == 0 and kdim % bk == 0
    nk = kdim // bk
    in_specs = [pl.BlockSpec((bm, bk), lambda j, i, k: (i, k)),
                pl.BlockSpec((bk, bn), lambda j, i, k: (k, j))]
    args = [a, w]
    if res is not None:
        in_specs.append(pl.BlockSpec((bm, bn), lambda j, i, k: (i, j)))
        args.append(res)
    scratch = [pltpu.VMEM((bm, bn), f32)] if nk > 1 else []
    return pl.pallas_call(
        functools.partial(_mm_kernel, nk=nk, has_res=res is not None, act=act),
        grid=(n // bn, m // bm, nk),
        in_specs=in_specs,
        out_specs=pl.BlockSpec((bm, bn), lambda j, i, k: (i, j)),
        out_shape=jax.ShapeDtypeStruct((m, n), out_dtype),
        scratch_shapes=scratch,
        compiler_params=_cparams(("parallel", "parallel", "arbitrary")),
        name="matmul",
    )(*args)


def _swiglu_kernel(a_ref, wg_ref, wu_ref, o_ref):
    a = a_ref[...]
    g = jnp.dot(a, wg_ref[...], preferred_element_type=f32)
    u = jnp.dot(a, wu_ref[...], preferred_element_type=f32)
    o_ref[...] = (g * jax.nn.sigmoid(g) * u).astype(o_ref.dtype)


def swiglu_up(a, wg, wu, bm=1024, bn=512):
    m, kdim = a.shape
    n = wg.shape[1]
    assert m % bm == 0 and n % bn == 0
    return pl.pallas_call(
        _swiglu_kernel,
        grid=(n // bn, m // bm),
        in_specs=[pl.BlockSpec((bm, kdim), lambda j, i: (i, 0)),
                  pl.BlockSpec((kdim, bn), lambda j, i: (0, j)),
                  pl.BlockSpec((kdim, bn), lambda j, i: (0, j))],
        out_specs=pl.BlockSpec((bm, bn), lambda j, i: (i, j)),
        out_shape=jax.ShapeDtypeStruct((m, n), bf16),
        compiler_params=_cparams(("parallel", "parallel")),
        name="swiglu_up",
    )(a, wg, wu)


def _rope(x, c, s_lo, s_hi):
    half = ROPE_DIM // 2
    return x * c + pltpu.roll(x, HEAD_DIM - half, 1) * s_lo + pltpu.roll(x, half, 1) * s_hi


def _qkv_post_kernel(z_ref, c_ref, slo_ref, shi_ref,
                     q_ref, kc_ref, vc_ref, ks_ref, vs_ref, kw_ref, vw_ref):
    c, s_lo, s_hi = c_ref[...], slo_ref[...], shi_ref[...]
    for h in range(NSA_HEADS):
        sl = slice(h * HEAD_DIM, (h + 1) * HEAD_DIM)
        q_ref[:, sl] = (_rope(z_ref[:, sl], c, s_lo, s_hi) * ATTN_SCALE).astype(q_ref.dtype)
    base = NSA_WIDTH
    for idx, (o_ref, roped) in enumerate(((kc_ref, True), (vc_ref, False), (ks_ref, True),
                                          (vs_ref, False), (kw_ref, True), (vw_ref, False))):
        for g in range(NSA_KV_HEADS):
            col = base + idx * KV_WIDTH + g * HEAD_DIM
            x = z_ref[:, col:col + HEAD_DIM]
            if roped:
                x = _rope(x, c, s_lo, s_hi)
            o_ref[:, g * HEAD_DIM:(g + 1) * HEAD_DIM] = x.astype(o_ref.dtype)


def _rope_tables(t_len):
    half = ROPE_DIM // 2
    inv_freq = ROPE_THETA ** (-jnp.arange(half, dtype=f32) * 2.0 / ROPE_DIM)
    ang = jnp.arange(t_len).astype(f32)[:, None] * inv_freq[None, :]
    cos, sin = jnp.cos(ang), jnp.sin(ang)
    ones = jnp.ones((t_len, HEAD_DIM - ROPE_DIM), f32)
    zeros = jnp.zeros((t_len, HEAD_DIM - ROPE_DIM), f32)
    zh = jnp.zeros((t_len, half), f32)
    c = jnp.concatenate([cos, cos, ones], axis=1)
    s_lo = jnp.concatenate([-sin, zh, zeros], axis=1)
    s_hi = jnp.concatenate([zh, sin, zeros], axis=1)
    return c, s_lo, s_hi


def qkv_post(z, t_len, tm=256):
    m = z.shape[0]
    c, s_lo, s_hi = _rope_tables(t_len)
    nt = t_len // tm
    tab = pl.BlockSpec((tm, HEAD_DIM), lambda i: (i % nt, 0))
    kv = pl.BlockSpec((tm, KV_WIDTH), lambda i: (i, 0))
    shp = lambda dt: jax.ShapeDtypeStruct((m, KV_WIDTH), dt)
    return pl.pallas_call(
        _qkv_post_kernel,
        grid=(m // tm,),
        in_specs=[pl.BlockSpec((tm, z.shape[1]), lambda i: (i, 0)), tab, tab, tab],
        out_specs=[pl.BlockSpec((tm, NSA_WIDTH), lambda i: (i, 0)), kv, kv, kv, kv, kv, kv],
        out_shape=[jax.ShapeDtypeStruct((m, NSA_WIDTH), bf16),
                   shp(f32), shp(f32), shp(bf16), shp(bf16), shp(bf16), shp(bf16)],
        compiler_params=_cparams(("parallel",)),
        name="qkv_post",
    )(z, c, s_lo, s_hi)


def _compress_kernel(a_ref, pe_ref, w1_ref, w2_ref, o_ref):
    half = CMP_STRIDE * HEAD_DIM
    a = a_ref[0, 0]
    n_chunks = a.shape[0]
    top = jnp.dot((a + pe_ref[:, :half]).astype(bf16), w1_ref[:half, :].astype(bf16),
                  preferred_element_type=f32)
    bot = jnp.dot((a + pe_ref[:, half:]).astype(bf16), w1_ref[half:, :].astype(bf16),
                  preferred_element_type=f32)
    pre = top + pltpu.roll(bot, n_chunks - 1, 0)
    hid = _gelu_tanh(pre)
    o_ref[0, 0] = jnp.dot(hid.astype(bf16), w2_ref[...].astype(bf16),
                          preferred_element_type=f32).astype(o_ref.dtype)


def compress(a, pe, w1, w2):
    b, g, n_chunks, width = a.shape
    return pl.pallas_call(
        _compress_kernel,
        grid=(b, g),
        in_specs=[pl.BlockSpec((1, 1, n_chunks, width), lambda i, j: (i, j, 0, 0)),
                  pl.BlockSpec((1, 2 * width), lambda i, j: (0, 0)),
                  pl.BlockSpec((2 * width, CMP_HIDDEN), lambda i, j: (0, 0)),
                  pl.BlockSpec((CMP_HIDDEN, HEAD_DIM), lambda i, j: (0, 0))],
        out_specs=pl.BlockSpec((1, 1, n_chunks, HEAD_DIM), lambda i, j: (i, j, 0, 0)),
        out_shape=jax.ShapeDtypeStruct((b, g, n_chunks, HEAD_DIM), bf16),
        compiler_params=_cparams(("parallel", "parallel")),
        name="compress",
    )(a, pe.reshape(1, 2 * width), w1.reshape(2 * width, CMP_HIDDEN), w2)


def _nsa_kernel(q_ref, kc_ref, vct_ref, ks_ref, vst_ref, kw_ref, vwt_ref, g_ref, o_ref,
                imp_ref, score_ref, sel_ref):
    tq, ck = ATT_TQ, ATT_CK
    nl = NSA_HPG * tq
    qi = pl.program_id(2)
    t0 = qi * tq
    qt = jnp.concatenate([q_ref[0, h] for h in range(NSA_HPG)], axis=1)
    n_cmp = kc_ref.shape[2]
    n_slc = sel_ref.shape[0]

    s = jnp.dot(kc_ref[0, 0], qt, preferred_element_type=f32)
    blk_end = lax.broadcasted_iota(jnp.int32, (n_cmp, nl), 0) * CMP_STRIDE + (CMP_BLOCK - 1)
    qpos4 = t0 + lax.broadcasted_iota(jnp.int32, (n_cmp, nl), 1) % tq
    valid = blk_end <= qpos4
    s = jnp.where(valid, s, NEG)
    mx = jnp.max(s, axis=0, keepdims=True)
    e = jnp.where(valid, jnp.exp(s - mx), 0.0)
    den = jnp.sum(e, axis=0, keepdims=True)
    p = e / jnp.where(den > 0.0, den, 1.0)
    o_cmp = jnp.dot(vct_ref[0, 0], p.astype(bf16), preferred_element_type=f32)
    imp_ref[...] = (p[:, 0:tq] + p[:, tq:2 * tq]) + p[:, 2 * tq:3 * tq] + p[:, 3 * tq:4 * tq]

    r = SLC_BLOCK // CMP_STRIDE
    rows = [imp_ref[pl.ds(k, n_slc, stride=r), :] for k in range(r)]
    blk = lax.broadcasted_iota(jnp.int32, (n_slc, tq), 0)
    prev = jnp.where(blk == 0, 0.0, pltpu.roll(rows[r - 1], 1, 0))
    score = (rows[0] + rows[1] + rows[2]) + 0.5 * rows[r - 1] + 0.5 * prev
    cur = (t0 + lax.broadcasted_iota(jnp.int32, (n_slc, tq), 1)) // SLC_BLOCK
    score = jnp.where((blk == cur) | (blk == 0), -NEG, jnp.where(blk > cur, NEG, score))
    score_ref[...] = score
    cnt = jnp.zeros((n_slc, tq), jnp.int32)
    for jp in range(n_slc):
        row = jnp.broadcast_to(score_ref[pl.ds(jp, 1), :], (n_slc, tq))
        beats = (row > score) | ((row == score) & (blk > jp))
        cnt = cnt + beats.astype(jnp.int32)
    sel_ref[...] = jnp.where(cnt < min(SLC_TOPK, n_slc), 1.0, 0.0)

    kpos0 = lax.broadcasted_iota(jnp.int32, (ck, tq), 0)
    qpos = t0 + lax.broadcasted_iota(jnp.int32, (ck, tq), 1)
    init = (jnp.full((1, nl), NEG, f32), jnp.zeros((1, nl), f32), jnp.zeros((HEAD_DIM, nl), f32))

    def flash_step(k_ref, vt_ref, c, mask, carry):
        m, l, acc = carry
        k = k_ref[0, pl.ds(pl.multiple_of(c * ck, ck), ck), :]
        sc = jnp.dot(k, qt, preferred_element_type=f32)
        sc = jnp.where(jnp.concatenate([mask] * NSA_HPG, axis=1), sc, NEG)
        m_new = jnp.maximum(m, jnp.max(sc, axis=0, keepdims=True))
        alpha = jnp.exp(m - m_new)
        pr = jnp.exp(sc - m_new)
        l = alpha * l + jnp.sum(pr, axis=0, keepdims=True)
        acc = alpha * acc + jnp.dot(vt_ref[0, 0, c], pr.astype(bf16), preferred_element_type=f32)
        return m_new, l, acc

    def slc_body(c, carry):
        per_chunk = ck // SLC_BLOCK
        selrows = jnp.concatenate(
            [jnp.broadcast_to(sel_ref[pl.ds(c * per_chunk + i, 1), :], (SLC_BLOCK, tq))
             for i in range(per_chunk)], axis=0)
        mask = (selrows > 0.5) & (kpos0 + c * ck <= qpos)
        return flash_step(ks_ref, vst_ref, c, mask, carry)

    c_last = (t0 + tq - 1) // ck
    _, l_s, acc_s = lax.fori_loop(0, c_last + 1, slc_body, init)
    o_slc = acc_s / l_s

    def win_body(i, carry):
        c = c_last - i
        kpos = kpos0 + c * ck
        mask = (kpos <= qpos) & (kpos > qpos - WINDOW)
        return flash_step(kw_ref, vwt_ref, c, mask, carry)

    c_first = jnp.maximum(t0 - (WINDOW - 1), 0) // ck
    _, l_w, acc_w = lax.fori_loop(0, c_last - c_first + 1, win_body, init)
    o_win = acc_w / l_w

    gate = jax.nn.sigmoid(g_ref[0, 0])
    for h in range(NSA_HPG):
        sl = slice(h * tq, (h + 1) * tq)
        o_h = (gate[3 * h:3 * h + 1, :] * o_cmp[:, sl] + gate[3 * h + 1:3 * h + 2, :] * o_slc[:, sl]
               + gate[3 * h + 2:3 * h + 3, :] * o_win[:, sl])
        o_ref[0, :, h * HEAD_DIM:(h + 1) * HEAD_DIM] = o_h.T.astype(o_ref.dtype)


def nsa_attention(qt, k_cmp, v_cmp_t, ks, vs_t, kw, vw_t, gates_t):
    b, _, _, t_len = qt.shape
    tq, ck = ATT_TQ, ATT_CK
    n_cmp = k_cmp.shape[2]
    n_slc = t_len // SLC_BLOCK
    kfull = pl.BlockSpec((1, t_len, HEAD_DIM), lambda i, g, q: (i, 0, g))
    vfull = pl.BlockSpec((1, 1, t_len // ck, HEAD_DIM, ck), lambda i, g, q: (i, g, 0, 0, 0))
    return pl.pallas_call(
        _nsa_kernel,
        grid=(b, NSA_KV_HEADS, t_len // tq),
        in_specs=[pl.BlockSpec((1, NSA_HPG, HEAD_DIM, tq), lambda i, g, q: (i, g, 0, q)),
                  pl.BlockSpec((1, 1, n_cmp, HEAD_DIM), lambda i, g, q: (i, g, 0, 0)),
                  pl.BlockSpec((1, 1, HEAD_DIM, n_cmp), lambda i, g, q: (i, g, 0, 0)),
                  kfull, vfull, kfull, vfull,
                  pl.BlockSpec((1, 1, 3 * NSA_HPG, tq), lambda i, g, q: (i, g, 0, q))],
        out_specs=pl.BlockSpec((1, tq, NSA_HPG * HEAD_DIM), lambda i, g, q: (i, q, g)),
        out_shape=jax.ShapeDtypeStruct((b, t_len, NSA_WIDTH), bf16),
        scratch_shapes=[pltpu.VMEM((n_cmp, tq), f32), pltpu.VMEM((n_slc, tq), f32),
                        pltpu.VMEM((n_slc, tq), f32)],
        compiler_params=_cparams(("parallel", "parallel", "arbitrary")),
        name="nsa_attention",
    )(qt, k_cmp, v_cmp_t, ks, vs_t, kw, vw_t, gates_t)


def _pool_kernel(x_ref, halo_ref, w_ref, scale_ref, o_ref, *, t_len):
    tm = x_ref.shape[0]
    i = pl.program_id(0)
    t_start = (i * tm) % t_len
    halo = jnp.where(t_start == 0, 0.0, halo_ref[...])
    tpos = t_start + lax.broadcasted_iota(jnp.int32, (tm, 1), 0)
    for gi, w_len in enumerate(POOL_WINDOWS):
        sl = slice(gi * POOL_GROUP, (gi + 1) * POOL_GROUP)
        x = x_ref[:, sl]
        acc = jnp.concatenate([halo[:, sl], x], axis=0)
        first = -POOL_HALO
        span = 1
        while span < w_len:
            acc = acc[span:, :] + acc[:-span, :]
            first += span
            span *= 2
        wsum = acc[-first:-first + tm, :]
        count = jnp.minimum(tpos + 1, w_len).astype(f32)
        y = (wsum / count - x).astype(bf16)
        y = jnp.dot(y, w_ref[gi].astype(bf16), preferred_element_type=f32)
        o_ref[:, sl] = (y * scale_ref[:, sl]).astype(o_ref.dtype)


def pool_mixer(x, w, scale, t_len, tm=256):
    m, c = x.shape
    hb = tm // POOL_HALO
    return pl.pallas_call(
        functools.partial(_pool_kernel, t_len=t_len),
        grid=(m // tm,),
        in_specs=[pl.BlockSpec((tm, c), lambda i: (i, 0)),
                  pl.BlockSpec((POOL_HALO, c), lambda i: (jnp.maximum(i * hb - 1, 0), 0)),
                  pl.BlockSpec(w.shape, lambda i: (0, 0, 0)),
                  pl.BlockSpec((1, c), lambda i: (0, 0))],
        out_specs=pl.BlockSpec((tm, c), lambda i: (i, 0)),
        out_shape=jax.ShapeDtypeStruct((m, c), bf16),
        compiler_params=_cparams(("parallel",)),
        name="pool_mixer",
    )(x, x, w, scale.reshape(1, c))


def _gmlp_kernel(u_ref, v_ref, lg_ref, lb_ref, ws_ref, bs_ref, o_ref, vn_ref):
    tc = v_ref.shape[0]
    vf = v_ref[...].astype(f32)
    mu = jnp.mean(vf, axis=-1, keepdims=True)
    d = vf - mu
    var = jnp.mean(d * d, axis=-1, keepdims=True)
    vn_ref[...] = (d * lax.rsqrt(var + EPS) * lg_ref[...] + lb_ref[...]).astype(bf16)
    gd = v_ref.shape[1] // GMLP_GROUPS
    tri = lax.broadcasted_iota(jnp.int32, (tc, tc), 0) >= lax.broadcasted_iota(jnp.int32, (tc, tc), 1)
    for g in range(GMLP_GROUPS):
        sl = slice(g * gd, (g + 1) * gd)
        w = jnp.where(tri, ws_ref[g], 0.0).astype(bf16)
        s = jnp.dot(w, vn_ref[:, sl], preferred_element_type=f32) + bs_ref[:, g:g + 1]
        o_ref[:, sl] = (u_ref[:, sl].astype(f32) * s).astype(o_ref.dtype)


def gmlp_gate(z, ln_g, ln_b, ws, bs):
    m, two_w = z.shape
    width = two_w // 2
    tc = GMLP_CHUNK
    row = pl.BlockSpec((1, width), lambda i: (0, 0))
    return pl.pallas_call(
        _gmlp_kernel,
        grid=(m // tc,),
        in_specs=[pl.BlockSpec((tc, width), lambda i: (i, 0)),
                  pl.BlockSpec((tc, width), lambda i: (i, 1)),
                  row, row,
                  pl.BlockSpec(ws.shape, lambda i: (0, 0, 0)),
                  pl.BlockSpec((tc, GMLP_GROUPS), lambda i: (0, 0))],
        out_specs=pl.BlockSpec((tc, width), lambda i: (i, 0)),
        out_shape=jax.ShapeDtypeStruct((m, width), bf16),
        scratch_shapes=[pltpu.VMEM((tc, width), bf16)],
        compiler_params=_cparams(("parallel",)),
        name="gmlp_gate",
    )(z, z, ln_g.reshape(1, width), ln_b.reshape(1, width), ws, bs.T)


def _ffn(x, norm_g, wg, wu, wd):
    hidden = wg.shape[1]
    pad = (-hidden) % FFN_PAD
    wg = jnp.pad(wg.astype(bf16), ((0, 0), (0, pad)))
    wu = jnp.pad(wu.astype(bf16), ((0, 0), (0, pad)))
    wd = jnp.pad(wd.astype(bf16), ((0, pad), (0, 0)))
    h = rms_norm(x, norm_g, bf16)
    gu = swiglu_up(h, wg, wu)
    return matmul(gu, wd, res=x)


def _even_layer(x, b, t_len, norm_g, w_in, pe_k, pe_v, w1_k, w2_k, w1_v, w2_v, pool_w, pool_scale, w_out):
    m = x.shape[0]
    g_cnt = NSA_KV_HEADS
    h = rms_norm(x, norm_g, bf16)
    qkv_w = NSA_WIDTH + 6 * KV_WIDTH
    w_qkv = w_in[:, :qkv_w].astype(bf16)
    w_gate = jnp.pad(w_in[:, qkv_w:qkv_w + N_GATES].astype(bf16), ((0, 0), (0, LANE - N_GATES)))
    w_pool = w_in[:, qkv_w + N_GATES:].astype(bf16)
    z = matmul(h, w_qkv, bn=1024)
    gates = matmul(h, w_gate, bn=LANE)[:, :N_GATES]
    pin = matmul(h, w_pool, bn=1024)

    q, kc, vc, ks, vs, kw, vw = qkv_post(z, t_len)

    def chunks(a):
        a = a.reshape(b, t_len // CMP_STRIDE, CMP_STRIDE, g_cnt, HEAD_DIM).transpose(0, 3, 1, 2, 4)
        return a.reshape(b, g_cnt, t_len // CMP_STRIDE, CMP_STRIDE * HEAD_DIM)

    k_cmp = compress(chunks(kc), pe_k, w1_k, w2_k)
    v_cmp_t = compress(chunks(vc), pe_v, w1_v, w2_v).transpose(0, 1, 3, 2)

    def v_layout(a):
        a = a.reshape(b, t_len // ATT_CK, ATT_CK, g_cnt, HEAD_DIM)
        return a.transpose(0, 3, 1, 4, 2)

    qt = q.reshape(b, t_len, NSA_HEADS, HEAD_DIM).transpose(0, 2, 3, 1)
    gates_t = gates.reshape(b, t_len, g_cnt, 3 * NSA_HPG).transpose(0, 2, 3, 1)
    o = nsa_attention(qt, k_cmp, v_cmp_t, ks.reshape(b, t_len, KV_WIDTH), v_layout(vs),
                      kw.reshape(b, t_len, KV_WIDTH), v_layout(vw), gates_t)
    y_pool = pool_mixer(pin, pool_w, pool_scale, t_len)
    cat = jnp.concatenate([o.reshape(m, NSA_WIDTH), y_pool], axis=1)
    return matmul(cat, w_out.astype(bf16), res=x)


def _odd_layer(x, norm_g, w_in, ln_g, ln_b, ws, bs, w_out):
    h = rms_norm(x, norm_g, bf16)
    z = matmul(h, w_in.astype(bf16), act="gelu", out_dtype=bf16)
    y = gmlp_gate(z, ln_g, ln_b, ws, bs)
    return matmul(y, w_out.astype(bf16), res=x)


def kernel(x, norm_mix_even, w_in_even, cmp_pe_k, cmp_pe_v, cmp_w1_k, cmp_w2_k, cmp_w1_v, cmp_w2_v, pool_w, pool_scale, w_out_even, norm_mix_odd, w_in_odd, gmlp_ln_g, gmlp_ln_b, gmlp_ws, gmlp_bs, w_out_odd, norm_ffn, w_ffn_gate, w_ffn_up, w_ffn_down, norm_final):
    b, t_len, d = x.shape
    depth = norm_ffn.shape[0]
    xf = x.reshape(b * t_len, d)
    for layer in range(depth):
        j = layer // 2
        if layer % 2 == 0:
            xf = _even_layer(xf, b, t_len, norm_mix_even[j], w_in_even[j], cmp_pe_k[j], cmp_pe_v[j],
                             cmp_w1_k[j], cmp_w2_k[j], cmp_w1_v[j], cmp_w2_v[j], pool_w[j],
                             pool_scale[j], w_out_even[j])
        else:
            xf = _odd_layer(xf, norm_mix_odd[j], w_in_odd[j], gmlp_ln_g[j], gmlp_ln_b[j], gmlp_ws[j],
                            gmlp_bs[j], w_out_odd[j])
        xf = _ffn(xf, norm_ffn[layer], w_ffn_gate[layer], w_ffn_up[layer], w_ffn_down[layer])
    return rms_norm(xf, norm_final, x.dtype).reshape(b, t_len, d)
```

```python
import functools

import jax
import jax.numpy as jnp
from jax import lax
from jax.experimental import pallas as pl
from jax.experimental.pallas import tpu as pltpu

HEAD_DIM = 128
NSA_HEADS = 16
NSA_KV_HEADS = 4
NSA_HPG = NSA_HEADS // NSA_KV_HEADS
NSA_WIDTH = NSA_HEADS * HEAD_DIM
KV_WIDTH = NSA_KV_HEADS * HEAD_DIM
CMP_STRIDE = 16
CMP_BLOCK = 32
CMP_HIDDEN = 256
SLC_BLOCK = 64
SLC_TOPK = 16
WINDOW = 512
N_GATES = 3 * NSA_HEADS
ATTN_SCALE = HEAD_DIM ** -0.5
ROPE_THETA = 500000.0
ROPE_DIM = HEAD_DIM // 4
POOL_WINDOWS = (2, 4, 8, 16)
POOL_GROUP = 512
POOL_HALO = 16
GMLP_CHUNK = 128
GMLP_GROUPS = 16
EPS = 1e-6
NEG = -1e30

LANE = 128
VMEM_LIMIT_BYTES = 56 * 1024 * 1024
ATT_TQ = 128
ATT_CK = 256
FFN_PAD = 512
MM_VMEM_BUDGET_BYTES = 48 * 1024 * 1024

bf16 = jnp.bfloat16
f32 = jnp.float32


def _cparams(sem):
    return pltpu.CompilerParams(dimension_semantics=sem, vmem_limit_bytes=VMEM_LIMIT_BYTES)


def _rms_kernel(x_ref, g_ref, o_ref):
    x = x_ref[...]
    ms = jnp.mean(x * x, axis=-1, keepdims=True)
    o_ref[...] = (x * lax.rsqrt(ms + EPS) * g_ref[...]).astype(o_ref.dtype)


def rms_norm(x, g, out_dtype, tm=256):
    m, d = x.shape
    return pl.pallas_call(
        _rms_kernel,
        grid=(m // tm,),
        in_specs=[pl.BlockSpec((tm, d), lambda i: (i, 0)), pl.BlockSpec((1, d), lambda i: (0, 0))],
        out_specs=pl.BlockSpec((tm, d), lambda i: (i, 0)),
        out_shape=jax.ShapeDtypeStruct((m, d), out_dtype),
        compiler_params=_cparams(("parallel",)),
        name="rms_norm",
    )(x, g.reshape(1, d))


def _gelu_tanh(x):
    return 0.5 * x * (1.0 + jnp.tanh(0.7978845608028654 * (x + 0.044715 * (x * x * x))))


def _mm_kernel(*refs, nk, has_res, act):
    if has_res:
        a_ref, w_ref, r_ref, o_ref = refs[:4]
        scratch = refs[4:]
    else:
        a_ref, w_ref, o_ref = refs[:3]
        r_ref = None
        scratch = refs[3:]

    def finish(acc):
        if act == "gelu":
            acc = _gelu_tanh(acc)
        if has_res:
            acc = acc + r_ref[...]
        o_ref[...] = acc.astype(o_ref.dtype)

    d = jnp.dot(a_ref[...], w_ref[...], preferred_element_type=f32)
    if nk == 1:
        finish(d)
        return
    acc_ref = scratch[0]
    k = pl.program_id(2)

    @pl.when(k == 0)
    def _():
        acc_ref[...] = d

    @pl.when(jnp.logical_and(k > 0, k < nk - 1))
    def _():
        acc_ref[...] += d

    @pl.when(k == nk - 1)
    def _():
        finish(acc_ref[...] + d)


def _pick_bk(kdim, bm, bn, out_bytes, has_res):
    fixed = bm * bn * (4 + 2 * out_bytes + (8 if has_res else 0))
    for nk in range(1, kdim // LANE + 1):
        bk = kdim // nk
        acc = bm * bn * 4 if nk > 1 else 0
        if kdim % nk == 0 and bk % LANE == 0 and fixed + acc + 4 * bk * (bm + bn) <= MM_VMEM_BUDGET_BYTES:
            return bk
    raise ValueError("no K tile fits")


def matmul(a, w, res=None, act=None, out_dtype=f32, bm=1024, bn=1024):
    m, kdim = a.shape
    _, n = w.shape
    bn = min(bn, n)
    bk = _pick_bk(kdim, bm, bn, jnp.dtype(out_dtype).itemsize, res is not None)
    assert m % bm == 0 and n % bn == 0 and kdim % bk == 0
    nk = kdim // bk
    in_specs = [pl.BlockSpec((bm, bk), lambda j, i, k: (i, k)),
                pl.BlockSpec((bk, bn), lambda j, i, k: (k, j))]
    args = [a, w]
    if res is not None:
        in_specs.append(pl.BlockSpec((bm, bn), lambda j, i, k: (i, j)))
        args.append(res)
    scratch = [pltpu.VMEM((bm, bn), f32)] if nk > 1 else []
    return pl.pallas_call(
        functools.partial(_mm_kernel, nk=nk, has_res=res is not None, act=act),
        grid=(n // bn, m // bm, nk),
        in_specs=in_specs,
        out_specs=pl.BlockSpec((bm, bn), lambda j, i, k: (i, j)),
        out_shape=jax.ShapeDtypeStruct((m, n), out_dtype),
        scratch_shapes=scratch,
        compiler_params=_cparams(("parallel", "parallel", "arbitrary")),
        name="matmul",
    )(*args)


def _mm_ws_kernel(*refs, nw, nt, nm, ck, bn, n_valid, act, has_res, has_side, side_rows):
    refs = list(refs)
    a_ref = refs.pop(0)
    w_refs = [refs.pop(0) for _ in range(nw)]
    r_ref = refs.pop(0) if has_res else None
    s_ref = refs.pop(0) if has_side else None
    o_ref = refs.pop(0)
    so_ref = refs.pop(0) if has_side else None
    wbf_ref = refs.pop(0)
    j = pl.program_id(0)
    i = pl.program_id(1)

    @pl.when(j < nt)
    def _():
        for t in range(nw):
            w = w_refs[t][...]
            if n_valid % bn:
                col = j * bn + lax.broadcasted_iota(jnp.int32, w.shape, 1)
                w = jnp.where(col < n_valid, w, 0.0)
            wbf_ref[(j % 2) * nw + t, pl.ds(pl.multiple_of(i * ck, ck), ck), :] = w.astype(bf16)

    @pl.when(j == 0)
    def _():
        o_ref[...] = jnp.zeros(o_ref.shape, o_ref.dtype)

    @pl.when(j > 0)
    def _():
        slot = ((j - 1) % 2) * nw
        a = a_ref[...]
        d = jnp.dot(a, wbf_ref[slot], preferred_element_type=f32)
        if nw == 2:
            u = jnp.dot(a, wbf_ref[slot + 1], preferred_element_type=f32)
            d = d * jax.nn.sigmoid(d) * u
        if act == "gelu":
            d = _gelu_tanh(d)
        if has_res:
            d = d + r_ref[...]
        o_ref[...] = d.astype(o_ref.dtype)

    if has_side:
        rs = s_ref.shape[0]
        slab = jnp.maximum(j - 1, 0) * nm + i * jnp.minimum(j, 1)
        row = slab * rs + lax.broadcasted_iota(jnp.int32, s_ref.shape, 0)
        so_ref[...] = jnp.where(row < side_rows, s_ref[...], 0.0).astype(so_ref.dtype)


def matmul_ws(a, ws, layer, n_cols, col0=0, res=None, act=None, out_dtype=f32, bm=1024, bn=1024,
              side=None, side_layer=0):
    m, kdim = a.shape
    nw = len(ws)
    nm = m // bm
    ck = kdim // nm
    nt = pl.cdiv(n_cols, bn)
    ct0 = col0 // bn
    assert m % bm == 0 and kdim % nm == 0 and ck % 8 == 0 and col0 % bn == 0
    a_spec = pl.BlockSpec((bm, kdim), lambda j, i: (i * jnp.minimum(j, 1), 0))
    w_spec = pl.BlockSpec((None, ck, bn), lambda j, i: (layer, i, ct0 + jnp.minimum(j, nt - 1)))
    o_map = lambda j, i: (i * jnp.minimum(j, 1), jnp.maximum(j - 1, 0))
    in_specs = [a_spec] + [w_spec] * nw
    args = [a] + list(ws)
    if res is not None:
        in_specs.append(pl.BlockSpec((bm, bn), o_map))
        args.append(res)
    out_specs = [pl.BlockSpec((bm, bn), o_map)]
    out_shape = [jax.ShapeDtypeStruct((m, nt * bn), out_dtype)]
    side_rows = 0
    if side is not None:
        _, side_rows, side_cols = side.shape
        n_slabs = nt * nm
        rs = -(-side_rows // n_slabs)
        rs = -(-rs // 16) * 16
        last = side_rows // rs - (1 if side_rows % rs == 0 else 0)
        s_map = lambda j, i: (jnp.maximum(j - 1, 0) * nm + i * jnp.minimum(j, 1), 0)
        in_specs.append(pl.BlockSpec((None, rs, side_cols),
                                     lambda j, i: (side_layer, jnp.minimum(s_map(j, i)[0], last), 0)))
        args.append(side)
        out_specs.append(pl.BlockSpec((rs, side_cols), s_map))
        out_shape.append(jax.ShapeDtypeStruct((n_slabs * rs, side_cols), bf16))
    outs = pl.pallas_call(
        functools.partial(_mm_ws_kernel, nw=nw, nt=nt, nm=nm, ck=ck, bn=bn, n_valid=n_cols, act=act,
                          has_res=res is not None, has_side=side is not None, side_rows=side_rows),
        grid=(nt + 1, nm),
        in_specs=in_specs,
        out_specs=out_specs,
        out_shape=out_shape,
        scratch_shapes=[pltpu.VMEM((2 * nw, kdim, bn), bf16)],
        compiler_params=_cparams(("arbitrary", "arbitrary")),
        name="matmul_ws",
    )(*args)
    return outs if side is not None else outs[0]


def _rope(x, c, s_lo, s_hi):
    half = ROPE_DIM // 2
    return x * c + pltpu.roll(x, HEAD_DIM - half, 1) * s_lo + pltpu.roll(x, half, 1) * s_hi


def _qkv_post_kernel(z_ref, c_ref, slo_ref, shi_ref,
                     q_ref, kc_ref, vc_ref, ks_ref, vs_ref, kw_ref, vw_ref):
    c, s_lo, s_hi = c_ref[...], slo_ref[...], shi_ref[...]
    for h in range(NSA_HEADS):
        sl = slice(h * HEAD_DIM, (h + 1) * HEAD_DIM)
        q_ref[:, sl] = (_rope(z_ref[:, sl], c, s_lo, s_hi) * ATTN_SCALE).astype(q_ref.dtype)
    base = NSA_WIDTH
    for idx, (o_ref, roped) in enumerate(((kc_ref, True), (vc_ref, False), (ks_ref, True),
                                          (vs_ref, False), (kw_ref, True), (vw_ref, False))):
        for g in range(NSA_KV_HEADS):
            col = base + idx * KV_WIDTH + g * HEAD_DIM
            x = z_ref[:, col:col + HEAD_DIM]
            if roped:
                x = _rope(x, c, s_lo, s_hi)
            o_ref[:, g * HEAD_DIM:(g + 1) * HEAD_DIM] = x.astype(o_ref.dtype)


def _rope_tables(t_len):
    half = ROPE_DIM // 2
    inv_freq = ROPE_THETA ** (-jnp.arange(half, dtype=f32) * 2.0 / ROPE_DIM)
    ang = jnp.arange(t_len).astype(f32)[:, None] * inv_freq[None, :]
    cos, sin = jnp.cos(ang), jnp.sin(ang)
    ones = jnp.ones((t_len, HEAD_DIM - ROPE_DIM), f32)
    zeros = jnp.zeros((t_len, HEAD_DIM - ROPE_DIM), f32)
    zh = jnp.zeros((t_len, half), f32)
    c = jnp.concatenate([cos, cos, ones], axis=1)
    s_lo = jnp.concatenate([-sin, zh, zeros], axis=1)
    s_hi = jnp.concatenate([zh, sin, zeros], axis=1)
    return c, s_lo, s_hi


def qkv_post(z, t_len, tm=256):
    m = z.shape[0]
    c, s_lo, s_hi = _rope_tables(t_len)
    nt = t_len // tm
    tab = pl.BlockSpec((tm, HEAD_DIM), lambda i: (i % nt, 0))
    kv = pl.BlockSpec((tm, KV_WIDTH), lambda i: (i, 0))
    shp = lambda dt: jax.ShapeDtypeStruct((m, KV_WIDTH), dt)
    return pl.pallas_call(
        _qkv_post_kernel,
        grid=(m // tm,),
        in_specs=[pl.BlockSpec((tm, z.shape[1]), lambda i: (i, 0)), tab, tab, tab],
        out_specs=[pl.BlockSpec((tm, NSA_WIDTH), lambda i: (i, 0)), kv, kv, kv, kv, kv, kv],
        out_shape=[jax.ShapeDtypeStruct((m, NSA_WIDTH), bf16),
                   shp(f32), shp(f32), shp(bf16), shp(bf16), shp(bf16), shp(bf16)],
        compiler_params=_cparams(("parallel",)),
        name="qkv_post",
    )(z, c, s_lo, s_hi)


def _compress_kernel(a_ref, pe_ref, w1_ref, w2_ref, o_ref):
    half = CMP_STRIDE * HEAD_DIM
    a = a_ref[0, 0]
    n_chunks = a.shape[0]
    top = jnp.dot((a + pe_ref[:, :half]).astype(bf16), w1_ref[:half, :].astype(bf16),
                  preferred_element_type=f32)
    bot = jnp.dot((a + pe_ref[:, half:]).astype(bf16), w1_ref[half:, :].astype(bf16),
                  preferred_element_type=f32)
    pre = top + pltpu.roll(bot, n_chunks - 1, 0)
    hid = _gelu_tanh(pre)
    o_ref[0, 0] = jnp.dot(hid.astype(bf16), w2_ref[...].astype(bf16),
                          preferred_element_type=f32).astype(o_ref.dtype)


def compress(a, pe, w1, w2):
    b, g, n_chunks, width = a.shape
    return pl.pallas_call(
        _compress_kernel,
        grid=(b, g),
        in_specs=[pl.BlockSpec((1, 1, n_chunks, width), lambda i, j: (i, j, 0, 0)),
                  pl.BlockSpec((1, 2 * width), lambda i, j: (0, 0)),
                  pl.BlockSpec((2 * width, CMP_HIDDEN), lambda i, j: (0, 0)),
                  pl.BlockSpec((CMP_HIDDEN, HEAD_DIM), lambda i, j: (0, 0))],
        out_specs=pl.BlockSpec((1, 1, n_chunks, HEAD_DIM), lambda i, j: (i, j, 0, 0)),
        out_shape=jax.ShapeDtypeStruct((b, g, n_chunks, HEAD_DIM), bf16),
        compiler_params=_cparams(("parallel", "parallel")),
        name="compress",
    )(a, pe.reshape(1, 2 * width), w1.reshape(2 * width, CMP_HIDDEN), w2)


def _nsa_kernel(q_ref, kc_ref, vct_ref, ks_ref, vst_ref, kw_ref, vwt_ref, g_ref, o_ref,
                imp_ref, score_ref, sel_ref):
    tq, ck = ATT_TQ, ATT_CK
    nl = NSA_HPG * tq
    qi = pl.program_id(2)
    t0 = qi * tq
    qt = jnp.concatenate([q_ref[0, h] for h in range(NSA_HPG)], axis=1)
    n_cmp = kc_ref.shape[2]
    n_slc = sel_ref.shape[0]

    s = jnp.dot(kc_ref[0, 0], qt, preferred_element_type=f32)
    blk_end = lax.broadcasted_iota(jnp.int32, (n_cmp, nl), 0) * CMP_STRIDE + (CMP_BLOCK - 1)
    qpos4 = t0 + lax.broadcasted_iota(jnp.int32, (n_cmp, nl), 1) % tq
    valid = blk_end <= qpos4
    s = jnp.where(valid, s, NEG)
    mx = jnp.max(s, axis=0, keepdims=True)
    e = jnp.where(valid, jnp.exp(s - mx), 0.0)
    den = jnp.sum(e, axis=0, keepdims=True)
    p = e / jnp.where(den > 0.0, den, 1.0)
    o_cmp = jnp.dot(vct_ref[0, 0], p.astype(bf16), preferred_element_type=f32)
    imp_ref[...] = (p[:, 0:tq] + p[:, tq:2 * tq]) + p[:, 2 * tq:3 * tq] + p[:, 3 * tq:4 * tq]

    r = SLC_BLOCK // CMP_STRIDE
    rows = [imp_ref[pl.ds(k, n_slc, stride=r), :] for k in range(r)]
    blk = lax.broadcasted_iota(jnp.int32, (n_slc, tq), 0)
    prev = jnp.where(blk == 0, 0.0, pltpu.roll(rows[r - 1], 1, 0))
    score = (rows[0] + rows[1] + rows[2]) + 0.5 * rows[r - 1] + 0.5 * prev
    cur = (t0 + lax.broadcasted_iota(jnp.int32, (n_slc, tq), 1)) // SLC_BLOCK
    score = jnp.where((blk == cur) | (blk == 0), -NEG, jnp.where(blk > cur, NEG, score))
    score_ref[...] = score
    cnt = jnp.zeros((n_slc, tq), jnp.int32)
    for jp in range(n_slc):
        row = jnp.broadcast_to(score_ref[pl.ds(jp, 1), :], (n_slc, tq))
        beats = (row > score) | ((row == score) & (blk > jp))
        cnt = cnt + beats.astype(jnp.int32)
    sel_ref[...] = jnp.where(cnt < min(SLC_TOPK, n_slc), 1.0, 0.0)

    kpos0 = lax.broadcasted_iota(jnp.int32, (ck, tq), 0)
    qpos = t0 + lax.broadcasted_iota(jnp.int32, (ck, tq), 1)
    init = (jnp.full((1, nl), NEG, f32), jnp.zeros((1, nl), f32), jnp.zeros((HEAD_DIM, nl), f32))

    def flash_step(k_ref, vt_ref, c, mask, carry):
        m, l, acc = carry
        k = k_ref[0, pl.ds(pl.multiple_of(c * ck, ck), ck), :]
        sc = jnp.dot(k, qt, preferred_element_type=f32)
        sc = jnp.where(jnp.concatenate([mask] * NSA_HPG, axis=1), sc, NEG)
        m_new = jnp.maximum(m, jnp.max(sc, axis=0, keepdims=True))
        alpha = jnp.exp(m - m_new)
        pr = jnp.exp(sc - m_new)
        l = alpha * l + jnp.sum(pr, axis=0, keepdims=True)
        acc = alpha * acc + jnp.dot(vt_ref[0, 0, c], pr.astype(bf16), preferred_element_type=f32)
        return m_new, l, acc

    def slc_body(c, carry):
        per_chunk = ck // SLC_BLOCK
        selrows = jnp.concatenate(
            [jnp.broadcast_to(sel_ref[pl.ds(c * per_chunk + i, 1), :], (SLC_BLOCK, tq))
             for i in range(per_chunk)], axis=0)
        mask = (selrows > 0.5) & (kpos0 + c * ck <= qpos)
        return flash_step(ks_ref, vst_ref, c, mask, carry)

    c_last = (t0 + tq - 1) // ck
    _, l_s, acc_s = lax.fori_loop(0, c_last + 1, slc_body, init)
    o_slc = acc_s / l_s

    def win_body(i, carry):
        c = c_last - i
        kpos = kpos0 + c * ck
        mask = (kpos <= qpos) & (kpos > qpos - WINDOW)
        return flash_step(kw_ref, vwt_ref, c, mask, carry)

    c_first = jnp.maximum(t0 - (WINDOW - 1), 0) // ck
    _, l_w, acc_w = lax.fori_loop(0, c_last - c_first + 1, win_body, init)
    o_win = acc_w / l_w

    gate = jax.nn.sigmoid(g_ref[0, 0])
    for h in range(NSA_HPG):
        sl = slice(h * tq, (h + 1) * tq)
        o_h = (gate[3 * h:3 * h + 1, :] * o_cmp[:, sl] + gate[3 * h + 1:3 * h + 2, :] * o_slc[:, sl]
               + gate[3 * h + 2:3 * h + 3, :] * o_win[:, sl])
        o_ref[0, :, h * HEAD_DIM:(h + 1) * HEAD_DIM] = o_h.T.astype(o_ref.dtype)


def nsa_attention(qt, k_cmp, v_cmp_t, ks, vs_t, kw, vw_t, gates_t):
    b, _, _, t_len = qt.shape
    tq, ck = ATT_TQ, ATT_CK
    n_cmp = k_cmp.shape[2]
    n_slc = t_len // SLC_BLOCK
    kfull = pl.BlockSpec((1, t_len, HEAD_DIM), lambda i, g, q: (i, 0, g))
    vfull = pl.BlockSpec((1, 1, t_len // ck, HEAD_DIM, ck), lambda i, g, q: (i, g, 0, 0, 0))
    return pl.pallas_call(
        _nsa_kernel,
        grid=(b, NSA_KV_HEADS, t_len // tq),
        in_specs=[pl.BlockSpec((1, NSA_HPG, HEAD_DIM, tq), lambda i, g, q: (i, g, 0, q)),
                  pl.BlockSpec((1, 1, n_cmp, HEAD_DIM), lambda i, g, q: (i, g, 0, 0)),
                  pl.BlockSpec((1, 1, HEAD_DIM, n_cmp), lambda i, g, q: (i, g, 0, 0)),
                  kfull, vfull, kfull, vfull,
                  pl.BlockSpec((1, 1, 3 * NSA_HPG, tq), lambda i, g, q: (i, g, 0, q))],
        out_specs=pl.BlockSpec((1, tq, NSA_HPG * HEAD_DIM), lambda i, g, q: (i, q, g)),
        out_shape=jax.ShapeDtypeStruct((b, t_len, NSA_WIDTH), bf16),
        scratch_shapes=[pltpu.VMEM((n_cmp, tq), f32), pltpu.VMEM((n_slc, tq), f32),
                        pltpu.VMEM((n_slc, tq), f32)],
        compiler_params=_cparams(("parallel", "parallel", "arbitrary")),
        name="nsa_attention",
    )(qt, k_cmp, v_cmp_t, ks, vs_t, kw, vw_t, gates_t)


def _pool_kernel(x_ref, halo_ref, w_ref, scale_ref, o_ref, *, t_len):
    tm = x_ref.shape[0]
    i = pl.program_id(0)
    t_start = (i * tm) % t_len
    halo = jnp.where(t_start == 0, 0.0, halo_ref[...])
    tpos = t_start + lax.broadcasted_iota(jnp.int32, (tm, 1), 0)
    for gi, w_len in enumerate(POOL_WINDOWS):
        sl = slice(gi * POOL_GROUP, (gi + 1) * POOL_GROUP)
        x = x_ref[:, sl]
        acc = jnp.concatenate([halo[:, sl], x], axis=0)
        first = -POOL_HALO
        span = 1
        while span < w_len:
            acc = acc[span:, :] + acc[:-span, :]
            first += span
            span *= 2
        wsum = acc[-first:-first + tm, :]
        count = jnp.minimum(tpos + 1, w_len).astype(f32)
        y = (wsum / count - x).astype(bf16)
        y = jnp.dot(y, w_ref[gi].astype(bf16), preferred_element_type=f32)
        o_ref[:, sl] = (y * scale_ref[:, sl]).astype(o_ref.dtype)


def pool_mixer(x, w, scale, t_len, tm=256):
    m, c = x.shape
    hb = tm // POOL_HALO
    return pl.pallas_call(
        functools.partial(_pool_kernel, t_len=t_len),
        grid=(m // tm,),
        in_specs=[pl.BlockSpec((tm, c), lambda i: (i, 0)),
                  pl.BlockSpec((POOL_HALO, c), lambda i: (jnp.maximum(i * hb - 1, 0), 0)),
                  pl.BlockSpec(w.shape, lambda i: (0, 0, 0)),
                  pl.BlockSpec((1, c), lambda i: (0, 0))],
        out_specs=pl.BlockSpec((tm, c), lambda i: (i, 0)),
        out_shape=jax.ShapeDtypeStruct((m, c), bf16),
        compiler_params=_cparams(("parallel",)),
        name="pool_mixer",
    )(x, x, w, scale.reshape(1, c))


def _gmlp_kernel(u_ref, v_ref, lg_ref, lb_ref, ws_ref, bs_ref, o_ref, vn_ref):
    tc = v_ref.shape[0]
    vf = v_ref[...].astype(f32)
    mu = jnp.mean(vf, axis=-1, keepdims=True)
    d = vf - mu
    var = jnp.mean(d * d, axis=-1, keepdims=True)
    vn_ref[...] = (d * lax.rsqrt(var + EPS) * lg_ref[...] + lb_ref[...]).astype(bf16)
    gd = v_ref.shape[1] // GMLP_GROUPS
    tri = lax.broadcasted_iota(jnp.int32, (tc, tc), 0) >= lax.broadcasted_iota(jnp.int32, (tc, tc), 1)
    for g in range(GMLP_GROUPS):
        sl = slice(g * gd, (g + 1) * gd)
        w = jnp.where(tri, ws_ref[g], 0.0).astype(bf16)
        s = jnp.dot(w, vn_ref[:, sl], preferred_element_type=f32) + bs_ref[:, g:g + 1]
        o_ref[:, sl] = (u_ref[:, sl].astype(f32) * s).astype(o_ref.dtype)


def gmlp_gate(z, ln_g, ln_b, ws, bs):
    m, two_w = z.shape
    width = two_w // 2
    tc = GMLP_CHUNK
    row = pl.BlockSpec((1, width), lambda i: (0, 0))
    return pl.pallas_call(
        _gmlp_kernel,
        grid=(m // tc,),
        in_specs=[pl.BlockSpec((tc, width), lambda i: (i, 0)),
                  pl.BlockSpec((tc, width), lambda i: (i, 1)),
                  row, row,
                  pl.BlockSpec(ws.shape, lambda i: (0, 0, 0)),
                  pl.BlockSpec((tc, GMLP_GROUPS), lambda i: (0, 0))],
        out_specs=pl.BlockSpec((tc, width), lambda i: (i, 0)),
        out_shape=jax.ShapeDtypeStruct((m, width), bf16),
        scratch_shapes=[pltpu.VMEM((tc, width), bf16)],
        compiler_params=_cparams(("parallel",)),
        name="gmlp_gate",
    )(z, z, ln_g.reshape(1, width), ln_b.reshape(1, width), ws, bs.T)


def _ffn(x, norm_g, wg, wu, wd, layer):
    h = rms_norm(x, norm_g, bf16)
    gu, wd_bf = matmul_ws(h, (wg, wu), layer, wg.shape[2], out_dtype=bf16, bn=FFN_PAD,
                          side=wd, side_layer=layer)
    return matmul(gu, wd_bf, res=x)


def _even_layer(x, b, t_len, norm_g, w_in, j, pe_k, pe_v, w1_k, w2_k, w1_v, w2_v, pool_w, pool_scale, w_out):
    m = x.shape[0]
    g_cnt = NSA_KV_HEADS
    h = rms_norm(x, norm_g, bf16)
    qkv_w = NSA_WIDTH + 6 * KV_WIDTH
    w_gate = jnp.pad(w_in[j, :, qkv_w:qkv_w + N_GATES].astype(bf16), ((0, 0), (0, LANE - N_GATES)))
    w_pool = w_in[j:j + 1, :, qkv_w + N_GATES:]
    z = matmul_ws(h, (w_in,), j, qkv_w)
    gates = matmul(h, w_gate, bn=LANE)[:, :N_GATES]
    pin = matmul_ws(h, (w_pool,), 0, w_pool.shape[2])

    q, kc, vc, ks, vs, kw, vw = qkv_post(z, t_len)

    def chunks(a):
        a = a.reshape(b, t_len // CMP_STRIDE, CMP_STRIDE, g_cnt, HEAD_DIM).transpose(0, 3, 1, 2, 4)
        return a.reshape(b, g_cnt, t_len // CMP_STRIDE, CMP_STRIDE * HEAD_DIM)

    k_cmp = compress(chunks(kc), pe_k, w1_k, w2_k)
    v_cmp_t = compress(chunks(vc), pe_v, w1_v, w2_v).transpose(0, 1, 3, 2)

    def v_layout(a):
        a = a.reshape(b, t_len // ATT_CK, ATT_CK, g_cnt, HEAD_DIM)
        return a.transpose(0, 3, 1, 4, 2)

    qt = q.reshape(b, t_len, NSA_HEADS, HEAD_DIM).transpose(0, 2, 3, 1)
    gates_t = gates.reshape(b, t_len, g_cnt, 3 * NSA_HPG).transpose(0, 2, 3, 1)
    o = nsa_attention(qt, k_cmp, v_cmp_t, ks.reshape(b, t_len, KV_WIDTH), v_layout(vs),
                      kw.reshape(b, t_len, KV_WIDTH), v_layout(vw), gates_t)
    y_pool = pool_mixer(pin, pool_w, pool_scale, t_len)
    cat = jnp.concatenate([o.reshape(m, NSA_WIDTH), y_pool], axis=1)
    return matmul_ws(cat, (w_out,), j, w_out.shape[2], res=x, bn=512)


def _odd_layer(x, norm_g, w_in, j, ln_g, ln_b, ws, bs, w_out):
    h = rms_norm(x, norm_g, bf16)
    z, w_out_bf = matmul_ws(h, (w_in,), j, w_in.shape[2], act="gelu", out_dtype=bf16,
                            side=w_out, side_layer=j)
    y = gmlp_gate(z, ln_g, ln_b, ws, bs)
    return matmul(y, w_out_bf, res=x)


def kernel(x, norm_mix_even, w_in_even, cmp_pe_k, cmp_pe_v, cmp_w1_k, cmp_w2_k, cmp_w1_v, cmp_w2_v, pool_w, pool_scale, w_out_even, norm_mix_odd, w_in_odd, gmlp_ln_g, gmlp_ln_b, gmlp_ws, gmlp_bs, w_out_odd, norm_ffn, w_ffn_gate, w_ffn_up, w_ffn_down, norm_final):
    b, t_len, d = x.shape
    depth = norm_ffn.shape[0]
    xf = x.reshape(b * t_len, d)
    for layer in range(depth):
        j = layer // 2
        if layer % 2 == 0:
            xf = _even_layer(xf, b, t_len, norm_mix_even[j], w_in_even, j, cmp_pe_k[j], cmp_pe_v[j],
                             cmp_w1_k[j], cmp_w2_k[j], cmp_w1_v[j], cmp_w2_v[j], pool_w[j],
                             pool_scale[j], w_out_even)
        else:
            xf = _odd_layer(xf, norm_mix_odd[j], w_in_odd, j, gmlp_ln_g[j], gmlp_ln_b[j], gmlp_ws[j],
                            gmlp_bs[j], w_out_odd)
        xf = _ffn(xf, norm_ffn[layer], w_ffn_gate, w_ffn_up, w_ffn_down, layer)
    return rms_norm(xf, norm_final, x.dtype).reshape(b, t_len, d)
```

```python
import functools

import jax
import jax.numpy as jnp
from jax import lax
from jax.experimental import pallas as pl
from jax.experimental.pallas import tpu as pltpu

HEAD_DIM = 128
NSA_HEADS = 16
NSA_KV_HEADS = 4
NSA_HPG = NSA_HEADS // NSA_KV_HEADS
NSA_WIDTH = NSA_HEADS * HEAD_DIM
KV_WIDTH = NSA_KV_HEADS * HEAD_DIM
CMP_STRIDE = 16
CMP_BLOCK = 32
CMP_HIDDEN = 256
SLC_BLOCK = 64
SLC_TOPK = 16
WINDOW = 512
N_GATES = 3 * NSA_HEADS
ATTN_SCALE = HEAD_DIM ** -0.5
ROPE_THETA = 500000.0
ROPE_DIM = HEAD_DIM // 4
POOL_WINDOWS = (2, 4, 8, 16)
POOL_GROUP = 512
POOL_HALO = 16
GMLP_CHUNK = 128
GMLP_GROUPS = 16
EPS = 1e-6
NEG = -1e30

LANE = 128
VMEM_LIMIT_BYTES = 56 * 1024 * 1024
ATT_TQ = 128
ATT_CK = 256
FFN_PAD = 512
MM_VMEM_BUDGET_BYTES = 48 * 1024 * 1024

bf16 = jnp.bfloat16
f32 = jnp.float32


def _cparams(sem):
    return pltpu.CompilerParams(dimension_semantics=sem, vmem_limit_bytes=VMEM_LIMIT_BYTES)


def _rms_kernel(x_ref, g_ref, o_ref):
    x = x_ref[...]
    ms = jnp.mean(x * x, axis=-1, keepdims=True)
    o_ref[...] = (x * lax.rsqrt(ms + EPS) * g_ref[...]).astype(o_ref.dtype)


def rms_norm(x, g, out_dtype, tm=256):
    m, d = x.shape
    return pl.pallas_call(
        _rms_kernel,
        grid=(m // tm,),
        in_specs=[pl.BlockSpec((tm, d), lambda i: (i, 0)), pl.BlockSpec((1, d), lambda i: (0, 0))],
        out_specs=pl.BlockSpec((tm, d), lambda i: (i, 0)),
        out_shape=jax.ShapeDtypeStruct((m, d), out_dtype),
        compiler_params=_cparams(("parallel",)),
        name="rms_norm",
    )(x, g.reshape(1, d))


def _gelu_tanh(x):
    return 0.5 * x * (1.0 + jnp.tanh(0.7978845608028654 * (x + 0.044715 * (x * x * x))))


def _mm_kernel(*refs, nk, has_res, act):
    if has_res:
        a_ref, w_ref, r_ref, o_ref = refs[:4]
        scratch = refs[4:]
    else:
        a_ref, w_ref, o_ref = refs[:3]
        r_ref = None
        scratch = refs[3:]

    def finish(acc):
        if act == "gelu":
            acc = _gelu_tanh(acc)
        if has_res:
            acc = acc + r_ref[...]
        o_ref[...] = acc.astype(o_ref.dtype)

    d = jnp.dot(a_ref[...], w_ref[...], preferred_element_type=f32)
    if nk == 1:
        finish(d)
        return
    acc_ref = scratch[0]
    k = pl.program_id(2)

    @pl.when(k == 0)
    def _():
        acc_ref[...] = d

    @pl.when(jnp.logical_and(k > 0, k < nk - 1))
    def _():
        acc_ref[...] += d

    @pl.when(k == nk - 1)
    def _():
        finish(acc_ref[...] + d)


def _pick_bk(kdim, bm, bn, out_bytes, has_res):
    fixed = bm * bn * (4 + 2 * out_bytes + (8 if has_res else 0))
    for nk in range(1, kdim // LANE + 1):
        bk = kdim // nk
        acc = bm * bn * 4 if nk > 1 else 0
        if kdim % nk == 0 and bk % LANE == 0 and fixed + acc + 4 * bk * (bm + bn) <= MM_VMEM_BUDGET_BYTES:
            return bk
    raise ValueError("no K tile fits")


def matmul(a, w, res=None, act=None, out_dtype=f32, bm=1024, bn=1024):
    m, kdim = a.shape
    _, n = w.shape
    bn = min(bn, n)
    bk = _pick_bk(kdim, bm, bn, jnp.dtype(out_dtype).itemsize, res is not None)
    assert m % bm == 0 and n % bn == 0 and kdim % bk == 0
    nk = kdim // bk
    in_specs = [pl.BlockSpec((bm, bk), lambda j, i, k: (i, k)),
                pl.BlockSpec((bk, bn), lambda j, i, k: (k, j))]
    args = [a, w]
    if res is not None:
        in_specs.append(pl.BlockSpec((bm, bn), lambda j, i, k: (i, j)))
        args.append(res)
    scratch = [pltpu.VMEM((bm, bn), f32)] if nk > 1 else []
    return pl.pallas_call(
        functools.partial(_mm_kernel, nk=nk, has_res=res is not None, act=act),
        grid=(n // bn, m // bm, nk),
        in_specs=in_specs,
        out_specs=pl.BlockSpec((bm, bn), lambda j, i, k: (i, j)),
        out_shape=jax.ShapeDtypeStruct((m, n), out_dtype),
        scratch_shapes=scratch,
        compiler_params=_cparams(("parallel", "parallel", "arbitrary")),
        name="matmul",
    )(*args)


def _mm_ws_kernel(*refs, nw, nt, nm, ck, bn, n_valid, act, has_res, has_side, side_rows):
    refs = list(refs)
    a_ref = refs.pop(0)
    w_refs = [refs.pop(0) for _ in range(nw)]
    r_ref = refs.pop(0) if has_res else None
    s_ref = refs.pop(0) if has_side else None
    o_ref = refs.pop(0)
    so_ref = refs.pop(0) if has_side else None
    wbf_ref = refs.pop(0)
    j = pl.program_id(0)
    i = pl.program_id(1)

    @pl.when(j < nt)
    def _():
        for t in range(nw):
            w = w_refs[t][...]
            if n_valid % bn:
                col = j * bn + lax.broadcasted_iota(jnp.int32, w.shape, 1)
                w = jnp.where(col < n_valid, w, 0.0)
            wbf_ref[(j % 2) * nw + t, pl.ds(pl.multiple_of(i * ck, ck), ck), :] = w.astype(bf16)

    @pl.when(j == 0)
    def _():
        o_ref[...] = jnp.zeros(o_ref.shape, o_ref.dtype)

    @pl.when(j > 0)
    def _():
        slot = ((j - 1) % 2) * nw
        a = a_ref[...]
        d = jnp.dot(a, wbf_ref[slot], preferred_element_type=f32)
        if nw == 2:
            u = jnp.dot(a, wbf_ref[slot + 1], preferred_element_type=f32)
            d = d * jax.nn.sigmoid(d) * u
        if act == "gelu":
            d = _gelu_tanh(d)
        if has_res:
            d = d + r_ref[...]
        o_ref[...] = d.astype(o_ref.dtype)

    if has_side:
        rs = s_ref.shape[0]
        slab = jnp.maximum(j - 1, 0) * nm + i * jnp.minimum(j, 1)
        row = slab * rs + lax.broadcasted_iota(jnp.int32, s_ref.shape, 0)
        so_ref[...] = jnp.where(row < side_rows, s_ref[...], 0.0).astype(so_ref.dtype)


def matmul_ws(a, ws, layer, n_cols, col0=0, res=None, act=None, out_dtype=f32, bm=1024, bn=1024,
              side=None, side_layer=0):
    m, kdim = a.shape
    nw = len(ws)
    nm = m // bm
    ck = kdim // nm
    nt = pl.cdiv(n_cols, bn)
    ct0 = col0 // bn
    assert m % bm == 0 and kdim % nm == 0 and ck % 8 == 0 and col0 % bn == 0
    a_spec = pl.BlockSpec((bm, kdim), lambda j, i: (i * jnp.minimum(j, 1), 0))
    w_spec = pl.BlockSpec((None, ck, bn), lambda j, i: (layer, i, ct0 + jnp.minimum(j, nt - 1)))
    o_map = lambda j, i: (i * jnp.minimum(j, 1), jnp.maximum(j - 1, 0))
    in_specs = [a_spec] + [w_spec] * nw
    args = [a] + list(ws)
    if res is not None:
        in_specs.append(pl.BlockSpec((bm, bn), o_map))
        args.append(res)
    out_specs = [pl.BlockSpec((bm, bn), o_map)]
    out_shape = [jax.ShapeDtypeStruct((m, nt * bn), out_dtype)]
    side_rows = 0
    if side is not None:
        _, side_rows, side_cols = side.shape
        n_slabs = nt * nm
        rs = -(-side_rows // n_slabs)
        rs = -(-rs // 16) * 16
        last = side_rows // rs - (1 if side_rows % rs == 0 else 0)
        s_map = lambda j, i: (jnp.maximum(j - 1, 0) * nm + i * jnp.minimum(j, 1), 0)
        in_specs.append(pl.BlockSpec((None, rs, side_cols),
                                     lambda j, i: (side_layer, jnp.minimum(s_map(j, i)[0], last), 0)))
        args.append(side)
        out_specs.append(pl.BlockSpec((rs, side_cols), s_map))
        out_shape.append(jax.ShapeDtypeStruct((n_slabs * rs, side_cols), bf16))
    outs = pl.pallas_call(
        functools.partial(_mm_ws_kernel, nw=nw, nt=nt, nm=nm, ck=ck, bn=bn, n_valid=n_cols, act=act,
                          has_res=res is not None, has_side=side is not None, side_rows=side_rows),
        grid=(nt + 1, nm),
        in_specs=in_specs,
        out_specs=out_specs,
        out_shape=out_shape,
        scratch_shapes=[pltpu.VMEM((2 * nw, kdim, bn), bf16)],
        compiler_params=_cparams(("arbitrary", "arbitrary")),
        name="matmul_ws",
    )(*args)
    return outs if side is not None else outs[0]


def _rope(x, c, s_lo, s_hi):
    half = ROPE_DIM // 2
    return x * c + pltpu.roll(x, HEAD_DIM - half, 1) * s_lo + pltpu.roll(x, half, 1) * s_hi


def _qkv_post_kernel(z_ref, c_ref, slo_ref, shi_ref,
                     q_ref, kc_ref, vc_ref, ks_ref, vs_ref, kw_ref, vw_ref):
    c, s_lo, s_hi = c_ref[...], slo_ref[...], shi_ref[...]
    for h in range(NSA_HEADS):
        sl = slice(h * HEAD_DIM, (h + 1) * HEAD_DIM)
        q_ref[:, sl] = (_rope(z_ref[:, sl], c, s_lo, s_hi) * ATTN_SCALE).astype(q_ref.dtype)
    base = NSA_WIDTH
    for idx, (o_ref, roped) in enumerate(((kc_ref, True), (vc_ref, False), (ks_ref, True),
                                          (vs_ref, False), (kw_ref, True), (vw_ref, False))):
        for g in range(NSA_KV_HEADS):
            col = base + idx * KV_WIDTH + g * HEAD_DIM
            x = z_ref[:, col:col + HEAD_DIM]
            if roped:
                x = _rope(x, c, s_lo, s_hi)
            o_ref[:, g * HEAD_DIM:(g + 1) * HEAD_DIM] = x.astype(o_ref.dtype)


def _rope_tables(t_len):
    half = ROPE_DIM // 2
    inv_freq = ROPE_THETA ** (-jnp.arange(half, dtype=f32) * 2.0 / ROPE_DIM)
    ang = jnp.arange(t_len).astype(f32)[:, None] * inv_freq[None, :]
    cos, sin = jnp.cos(ang), jnp.sin(ang)
    ones = jnp.ones((t_len, HEAD_DIM - ROPE_DIM), f32)
    zeros = jnp.zeros((t_len, HEAD_DIM - ROPE_DIM), f32)
    zh = jnp.zeros((t_len, half), f32)
    c = jnp.concatenate([cos, cos, ones], axis=1)
    s_lo = jnp.concatenate([-sin, zh, zeros], axis=1)
    s_hi = jnp.concatenate([zh, sin, zeros], axis=1)
    return c, s_lo, s_hi


def qkv_post(z, t_len, tm=256):
    m = z.shape[0]
    c, s_lo, s_hi = _rope_tables(t_len)
    nt = t_len // tm
    tab = pl.BlockSpec((tm, HEAD_DIM), lambda i: (i % nt, 0))
    kv = pl.BlockSpec((tm, KV_WIDTH), lambda i: (i, 0))
    shp = lambda dt: jax.ShapeDtypeStruct((m, KV_WIDTH), dt)
    return pl.pallas_call(
        _qkv_post_kernel,
        grid=(m // tm,),
        in_specs=[pl.BlockSpec((tm, z.shape[1]), lambda i: (i, 0)), tab, tab, tab],
        out_specs=[pl.BlockSpec((tm, NSA_WIDTH), lambda i: (i, 0)), kv, kv, kv, kv, kv, kv],
        out_shape=[jax.ShapeDtypeStruct((m, NSA_WIDTH), bf16),
                   shp(f32), shp(f32), shp(bf16), shp(bf16), shp(bf16), shp(bf16)],
        compiler_params=_cparams(("parallel",)),
        name="qkv_post",
    )(z, c, s_lo, s_hi)


def _compress_kernel(a_ref, pe_ref, w1_ref, w2_ref, o_ref):
    half = CMP_STRIDE * HEAD_DIM
    a = a_ref[0, 0]
    n_chunks = a.shape[0]
    top = jnp.dot((a + pe_ref[:, :half]).astype(bf16), w1_ref[:half, :].astype(bf16),
                  preferred_element_type=f32)
    bot = jnp.dot((a + pe_ref[:, half:]).astype(bf16), w1_ref[half:, :].astype(bf16),
                  preferred_element_type=f32)
    pre = top + pltpu.roll(bot, n_chunks - 1, 0)
    hid = _gelu_tanh(pre)
    o_ref[0, 0] = jnp.dot(hid.astype(bf16), w2_ref[...].astype(bf16),
                          preferred_element_type=f32).astype(o_ref.dtype)


def compress(a, pe, w1, w2):
    b, g, n_chunks, width = a.shape
    return pl.pallas_call(
        _compress_kernel,
        grid=(b, g),
        in_specs=[pl.BlockSpec((1, 1, n_chunks, width), lambda i, j: (i, j, 0, 0)),
                  pl.BlockSpec((1, 2 * width), lambda i, j: (0, 0)),
                  pl.BlockSpec((2 * width, CMP_HIDDEN), lambda i, j: (0, 0)),
                  pl.BlockSpec((CMP_HIDDEN, HEAD_DIM), lambda i, j: (0, 0))],
        out_specs=pl.BlockSpec((1, 1, n_chunks, HEAD_DIM), lambda i, j: (i, j, 0, 0)),
        out_shape=jax.ShapeDtypeStruct((b, g, n_chunks, HEAD_DIM), bf16),
        compiler_params=_cparams(("parallel", "parallel")),
        name="compress",
    )(a, pe.reshape(1, 2 * width), w1.reshape(2 * width, CMP_HIDDEN), w2)


def _nsa_kernel(q_ref, kc_ref, vct_ref, ks_ref, vst_ref, kw_ref, vwt_ref, g_ref, o_ref,
                imp_ref, score_ref, sel_ref):
    tq, ck = ATT_TQ, ATT_CK
    nl = NSA_HPG * tq
    qi = pl.program_id(2)
    t0 = qi * tq
    qt = jnp.concatenate([q_ref[0, h] for h in range(NSA_HPG)], axis=1)
    n_cmp = kc_ref.shape[2]
    n_slc = sel_ref.shape[0]

    kpos0 = lax.broadcasted_iota(jnp.int32, (ck, tq), 0)
    qpos = t0 + lax.broadcasted_iota(jnp.int32, (ck, tq), 1)
    c_last = (t0 + tq - 1) // ck

    def raw_scores(k_ref, c):
        k = k_ref[0, pl.ds(pl.multiple_of(c * ck, ck), ck), :]
        return jnp.dot(k, qt, preferred_element_type=f32)

    def masked(sc, mask):
        return jnp.where(jnp.concatenate([mask] * NSA_HPG, axis=1), sc, NEG)

    s = jnp.dot(kc_ref[0, 0], qt, preferred_element_type=f32)
    n_win = 1 + -(-(WINDOW - 1) // ck)
    c_win = [c_last - back for back in range(n_win)]
    s_win = [raw_scores(kw_ref, jnp.maximum(c, 0)) for c in c_win]

    def pair_scores(c):
        return raw_scores(ks_ref, c), raw_scores(ks_ref, c + 1)

    blk_end = lax.broadcasted_iota(jnp.int32, (n_cmp, nl), 0) * CMP_STRIDE + (CMP_BLOCK - 1)
    qpos4 = t0 + lax.broadcasted_iota(jnp.int32, (n_cmp, nl), 1) % tq
    valid = blk_end <= qpos4
    s = jnp.where(valid, s, NEG)
    mx = jnp.max(s, axis=0, keepdims=True)
    e = jnp.where(valid, jnp.exp(s - mx), 0.0)
    den = jnp.sum(e, axis=0, keepdims=True)
    p = e / jnp.where(den > 0.0, den, 1.0)
    o_cmp = jnp.dot(vct_ref[0, 0], p.astype(bf16), preferred_element_type=f32)
    imp_ref[...] = (p[:, 0:tq] + p[:, tq:2 * tq]) + p[:, 2 * tq:3 * tq] + p[:, 3 * tq:4 * tq]

    for i, c in enumerate(c_win):
        kpos = kpos0 + c * ck
        s_win[i] = masked(s_win[i], (kpos <= qpos) & (kpos > qpos - WINDOW) & (kpos >= 0))
    m_w = functools.reduce(jnp.maximum, [jnp.max(sw, axis=0, keepdims=True) for sw in s_win])
    p_win = [jnp.exp(sw - m_w) for sw in s_win]
    l_w = functools.reduce(jnp.add, [jnp.sum(pw, axis=0, keepdims=True) for pw in p_win])
    acc_w = functools.reduce(jnp.add, [
        jnp.dot(vwt_ref[0, 0, jnp.maximum(c, 0)], pw.astype(bf16), preferred_element_type=f32)
        for c, pw in zip(c_win, p_win)])
    o_win = acc_w / l_w

    r = SLC_BLOCK // CMP_STRIDE
    rows = [imp_ref[pl.ds(k, n_slc, stride=r), :] for k in range(r)]
    blk = lax.broadcasted_iota(jnp.int32, (n_slc, tq), 0)
    prev = jnp.where(blk == 0, 0.0, pltpu.roll(rows[r - 1], 1, 0))
    score = (rows[0] + rows[1] + rows[2]) + 0.5 * rows[r - 1] + 0.5 * prev
    cur = (t0 + lax.broadcasted_iota(jnp.int32, (n_slc, tq), 1)) // SLC_BLOCK
    score = jnp.where((blk == cur) | (blk == 0), -NEG, jnp.where(blk > cur, NEG, score))
    score_ref[...] = score
    cnt = jnp.zeros((n_slc, tq), jnp.int32)
    for jp in range(n_slc):
        row = jnp.broadcast_to(score_ref[pl.ds(jp, 1), :], (n_slc, tq))
        beats = (row > score) | ((row == score) & (blk > jp))
        cnt = cnt + beats.astype(jnp.int32)
    sel_ref[...] = jnp.where(cnt < min(SLC_TOPK, n_slc), 1.0, 0.0)

    init = (jnp.full((1, nl), NEG, f32), jnp.zeros((1, nl), f32), jnp.zeros((HEAD_DIM, nl), f32))

    def flash_update(c, sc, carry):
        m, l, acc = carry
        per_chunk = ck // SLC_BLOCK
        selrows = jnp.concatenate(
            [jnp.broadcast_to(sel_ref[pl.ds(c * per_chunk + i, 1), :], (SLC_BLOCK, tq))
             for i in range(per_chunk)], axis=0)
        sc = masked(sc, (selrows > 0.5) & (kpos0 + c * ck <= qpos))
        m_new = jnp.maximum(m, jnp.max(sc, axis=0, keepdims=True))
        alpha = jnp.exp(m - m_new)
        pr = jnp.exp(sc - m_new)
        l = alpha * l + jnp.sum(pr, axis=0, keepdims=True)
        acc = alpha * acc + jnp.dot(vst_ref[0, 0, c], pr.astype(bf16), preferred_element_type=f32)
        return m_new, l, acc

    def pair_body(i, carry):
        s_a, s_b = pair_scores(2 * i)
        return flash_update(2 * i, s_a, carry[0]), flash_update(2 * i + 1, s_b, carry[1])

    (m_a, l_a, acc_a), (m_b, l_b, acc_b) = lax.fori_loop(0, (c_last + 2) // 2, pair_body, (init, init))
    m_s = jnp.maximum(m_a, m_b)
    w_a, w_b = jnp.exp(m_a - m_s), jnp.exp(m_b - m_s)
    o_slc = (w_a * acc_a + w_b * acc_b) / (w_a * l_a + w_b * l_b)

    gate = jax.nn.sigmoid(g_ref[0, 0])
    for h in range(NSA_HPG):
        sl = slice(h * tq, (h + 1) * tq)
        o_h = (gate[3 * h:3 * h + 1, :] * o_cmp[:, sl] + gate[3 * h + 1:3 * h + 2, :] * o_slc[:, sl]
               + gate[3 * h + 2:3 * h + 3, :] * o_win[:, sl])
        o_ref[0, :, h * HEAD_DIM:(h + 1) * HEAD_DIM] = o_h.T.astype(o_ref.dtype)


def nsa_attention(qt, k_cmp, v_cmp_t, ks, vs_t, kw, vw_t, gates_t):
    b, _, _, t_len = qt.shape
    tq, ck = ATT_TQ, ATT_CK
    n_cmp = k_cmp.shape[2]
    n_slc = t_len // SLC_BLOCK
    assert (t_len // ck) % 2 == 0
    kfull =pl.BlockSpec((1, t_len, HEAD_DIM), lambda i, g, q: (i, 0, g))
    vfull = pl.BlockSpec((1, 1, t_len // ck, HEAD_DIM, ck), lambda i, g, q: (i, g, 0, 0, 0))
    return pl.pallas_call(
        _nsa_kernel,
        grid=(b, NSA_KV_HEADS, t_len // tq),
        in_specs=[pl.BlockSpec((1, NSA_HPG, HEAD_DIM, tq), lambda i, g, q: (i, g, 0, q)),
                  pl.BlockSpec((1, 1, n_cmp, HEAD_DIM), lambda i, g, q: (i, g, 0, 0)),
                  pl.BlockSpec((1, 1, HEAD_DIM, n_cmp), lambda i, g, q: (i, g, 0, 0)),
                  kfull, vfull, kfull, vfull,
                  pl.BlockSpec((1, 1, 3 * NSA_HPG, tq), lambda i, g, q: (i, g, 0, q))],
        out_specs=pl.BlockSpec((1, tq, NSA_HPG * HEAD_DIM), lambda i, g, q: (i, q, g)),
        out_shape=jax.ShapeDtypeStruct((b, t_len, NSA_WIDTH), bf16),
        scratch_shapes=[pltpu.VMEM((n_cmp, tq), f32), pltpu.VMEM((n_slc, tq), f32),
                        pltpu.VMEM((n_slc, tq), f32)],
        compiler_params=_cparams(("parallel", "parallel", "arbitrary")),
        name="nsa_attention",
    )(qt, k_cmp, v_cmp_t, ks, vs_t, kw, vw_t, gates_t)


def _pool_kernel(x_ref, halo_ref, w_ref, scale_ref, o_ref, *, t_len):
    tm = x_ref.shape[0]
    i = pl.program_id(0)
    t_start = (i * tm) % t_len
    halo = jnp.where(t_start == 0, 0.0, halo_ref[...])
    tpos = t_start + lax.broadcasted_iota(jnp.int32, (tm, 1), 0)
    for gi, w_len in enumerate(POOL_WINDOWS):
        sl = slice(gi * POOL_GROUP, (gi + 1) * POOL_GROUP)
        x = x_ref[:, sl]
        acc = jnp.concatenate([halo[:, sl], x], axis=0)
        first = -POOL_HALO
        span = 1
        while span < w_len:
            acc = acc[span:, :] + acc[:-span, :]
            first += span
            span *= 2
        wsum = acc[-first:-first + tm, :]
        count = jnp.minimum(tpos + 1, w_len).astype(f32)
        y = (wsum / count - x).astype(bf16)
        y = jnp.dot(y, w_ref[gi].astype(bf16), preferred_element_type=f32)
        o_ref[:, sl] = (y * scale_ref[:, sl]).astype(o_ref.dtype)


def pool_mixer(x, w, scale, t_len, tm=256):
    m, c = x.shape
    hb = tm // POOL_HALO
    return pl.pallas_call(
        functools.partial(_pool_kernel, t_len=t_len),
        grid=(m // tm,),
        in_specs=[pl.BlockSpec((tm, c), lambda i: (i, 0)),
                  pl.BlockSpec((POOL_HALO, c), lambda i: (jnp.maximum(i * hb - 1, 0), 0)),
                  pl.BlockSpec(w.shape, lambda i: (0, 0, 0)),
                  pl.BlockSpec((1, c), lambda i: (0, 0))],
        out_specs=pl.BlockSpec((tm, c), lambda i: (i, 0)),
        out_shape=jax.ShapeDtypeStruct((m, c), bf16),
        compiler_params=_cparams(("parallel",)),
        name="pool_mixer",
    )(x, x, w, scale.reshape(1, c))


def _gmlp_kernel(u_ref, v_ref, lg_ref, lb_ref, ws_ref, bs_ref, o_ref, vn_ref):
    tc = v_ref.shape[0]
    vf = v_ref[...].astype(f32)
    mu = jnp.mean(vf, axis=-1, keepdims=True)
    d = vf - mu
    var = jnp.mean(d * d, axis=-1, keepdims=True)
    vn_ref[...] = (d * lax.rsqrt(var + EPS) * lg_ref[...] + lb_ref[...]).astype(bf16)
    gd = v_ref.shape[1] // GMLP_GROUPS
    tri = lax.broadcasted_iota(jnp.int32, (tc, tc), 0) >= lax.broadcasted_iota(jnp.int32, (tc, tc), 1)
    for g in range(GMLP_GROUPS):
        sl = slice(g * gd, (g + 1) * gd)
        w = jnp.where(tri, ws_ref[g], 0.0).astype(bf16)
        s = jnp.dot(w, vn_ref[:, sl], preferred_element_type=f32) + bs_ref[:, g:g + 1]
        o_ref[:, sl] = (u_ref[:, sl].astype(f32) * s).astype(o_ref.dtype)


def gmlp_gate(z, ln_g, ln_b, ws, bs):
    m, two_w = z.shape
    width = two_w // 2
    tc = GMLP_CHUNK
    row = pl.BlockSpec((1, width), lambda i: (0, 0))
    return pl.pallas_call(
        _gmlp_kernel,
        grid=(m // tc,),
        in_specs=[pl.BlockSpec((tc, width), lambda i: (i, 0)),
                  pl.BlockSpec((tc, width), lambda i: (i, 1)),
                  row, row,
                  pl.BlockSpec(ws.shape, lambda i: (0, 0, 0)),
                  pl.BlockSpec((tc, GMLP_GROUPS), lambda i: (0, 0))],
        out_specs=pl.BlockSpec((tc, width), lambda i: (i, 0)),
        out_shape=jax.ShapeDtypeStruct((m, width), bf16),
        scratch_shapes=[pltpu.VMEM((tc, width), bf16)],
        compiler_params=_cparams(("parallel",)),
        name="gmlp_gate",
    )(z, z, ln_g.reshape(1, width), ln_b.reshape(1, width), ws, bs.T)


def _ffn(x, norm_g, wg, wu, wd, layer):
    h = rms_norm(x, norm_g, bf16)
    gu, wd_bf = matmul_ws(h, (wg, wu), layer, wg.shape[2], out_dtype=bf16, bn=FFN_PAD,
                          side=wd, side_layer=layer)
    return matmul(gu, wd_bf, res=x)


def _even_layer(x, b, t_len, norm_g, w_in, j, pe_k, pe_v, w1_k, w2_k, w1_v, w2_v, pool_w, pool_scale, w_out):
    m = x.shape[0]
    g_cnt = NSA_KV_HEADS
    h = rms_norm(x, norm_g, bf16)
    qkv_w = NSA_WIDTH + 6 * KV_WIDTH
    w_gate = jnp.pad(w_in[j, :, qkv_w:qkv_w + N_GATES].astype(bf16), ((0, 0), (0, LANE - N_GATES)))
    w_pool = w_in[j:j + 1, :, qkv_w + N_GATES:]
    z = matmul_ws(h, (w_in,), j, qkv_w)
    gates = matmul(h, w_gate, bn=LANE)[:, :N_GATES]
    pin = matmul_ws(h, (w_pool,), 0, w_pool.shape[2])

    q, kc, vc, ks, vs, kw, vw = qkv_post(z, t_len)

    def chunks(a):
        a = a.reshape(b, t_len // CMP_STRIDE, CMP_STRIDE, g_cnt, HEAD_DIM).transpose(0, 3, 1, 2, 4)
        return a.reshape(b, g_cnt, t_len // CMP_STRIDE, CMP_STRIDE * HEAD_DIM)

    k_cmp = compress(chunks(kc), pe_k, w1_k, w2_k)
    v_cmp_t = compress(chunks(vc), pe_v, w1_v, w2_v).transpose(0, 1, 3, 2)

    def v_layout(a):
        a = a.reshape(b, t_len // ATT_CK, ATT_CK, g_cnt, HEAD_DIM)
        return a.transpose(0, 3, 1, 4, 2)

    qt = q.reshape(b, t_len, NSA_HEADS, HEAD_DIM).transpose(0, 2, 3, 1)
    gates_t = gates.reshape(b, t_len, g_cnt, 3 * NSA_HPG).transpose(0, 2, 3, 1)
    o = nsa_attention(qt, k_cmp, v_cmp_t, ks.reshape(b, t_len, KV_WIDTH), v_layout(vs),
                      kw.reshape(b, t_len, KV_WIDTH), v_layout(vw), gates_t)
    y_pool = pool_mixer(pin, pool_w, pool_scale, t_len)
    cat = jnp.concatenate([o.reshape(m, NSA_WIDTH), y_pool], axis=1)
    return matmul_ws(cat, (w_out,), j, w_out.shape[2], res=x, bn=512)


def _odd_layer(x, norm_g, w_in, j, ln_g, ln_b, ws, bs, w_out):
    h = rms_norm(x, norm_g, bf16)
    z, w_out_bf = matmul_ws(h, (w_in,), j, w_in.shape[2], act="gelu", out_dtype=bf16,
                            side=w_out, side_layer=j)
    y = gmlp_gate(z, ln_g, ln_b, ws, bs)
    return matmul(y, w_out_bf, res=x)


def kernel(x, norm_mix_even, w_in_even, cmp_pe_k, cmp_pe_v, cmp_w1_k, cmp_w2_k, cmp_w1_v, cmp_w2_v, pool_w, pool_scale, w_out_even, norm_mix_odd, w_in_odd, gmlp_ln_g, gmlp_ln_b, gmlp_ws, gmlp_bs, w_out_odd, norm_ffn, w_ffn_gate, w_ffn_up, w_ffn_down, norm_final):
    b, t_len, d = x.shape
    depth = norm_ffn.shape[0]
    xf = x.reshape(b * t_len, d)
    for layer in range(depth):
        j = layer // 2
        if layer % 2 == 0:
            xf = _even_layer(xf, b, t_len, norm_mix_even[j], w_in_even, j, cmp_pe_k[j], cmp_pe_v[j],
                             cmp_w1_k[j], cmp_w2_k[j], cmp_w1_v[j], cmp_w2_v[j], pool_w[j],
                             pool_scale[j], w_out_even)
        else:
            xf = _odd_layer(xf, norm_mix_odd[j], w_in_odd, j, gmlp_ln_g[j], gmlp_ln_b[j], gmlp_ws[j],
                            gmlp_bs[j], w_out_odd)
        xf = _ffn(xf, norm_ffn[layer], w_ffn_gate, w_ffn_up, w_ffn_down, layer)
    return rms_norm(xf, norm_final, x.dtype).reshape(b, t_len, d)
```

```python
import functools

import jax
import jax.numpy as jnp
from jax import lax
from jax.experimental import pallas as pl
from jax.experimental.pallas import tpu as pltpu

HEAD_DIM = 128
NSA_HEADS = 16
NSA_KV_HEADS = 4
NSA_HPG = NSA_HEADS // NSA_KV_HEADS
NSA_WIDTH = NSA_HEADS * HEAD_DIM
KV_WIDTH = NSA_KV_HEADS * HEAD_DIM
CMP_STRIDE = 16
CMP_BLOCK = 32
CMP_HIDDEN = 256
SLC_BLOCK = 64
SLC_TOPK = 16
WINDOW = 512
N_GATES = 3 * NSA_HEADS
ATTN_SCALE = HEAD_DIM ** -0.5
ROPE_THETA = 500000.0
ROPE_DIM = HEAD_DIM // 4
POOL_WINDOWS = (2, 4, 8, 16)
POOL_GROUP = 512
POOL_HALO = 16
GMLP_CHUNK = 128
GMLP_GROUPS = 16
EPS = 1e-6
NEG = -1e30

LANE = 128
MXU_COLS = 256
VMEM_LIMIT_BYTES = 56 * 1024 * 1024
ATT_TQ = 128
ATT_CK = 256
FFN_PAD = 512
MM_VMEM_BUDGET_BYTES = 48 * 1024 * 1024

bf16 = jnp.bfloat16
f32 = jnp.float32


def _cparams(sem):
    return pltpu.CompilerParams(dimension_semantics=sem, vmem_limit_bytes=VMEM_LIMIT_BYTES)


def _rms_kernel(x_ref, g_ref, o_ref):
    x = x_ref[...]
    ms = jnp.mean(x * x, axis=-1, keepdims=True)
    o_ref[...] = (x * lax.rsqrt(ms + EPS) * g_ref[...]).astype(o_ref.dtype)


def rms_norm(x, g, out_dtype, tm=256):
    m, d = x.shape
    return pl.pallas_call(
        _rms_kernel,
        grid=(m // tm,),
        in_specs=[pl.BlockSpec((tm, d), lambda i: (i, 0)), pl.BlockSpec((1, d), lambda i: (0, 0))],
        out_specs=pl.BlockSpec((tm, d), lambda i: (i, 0)),
        out_shape=jax.ShapeDtypeStruct((m, d), out_dtype),
        compiler_params=_cparams(("parallel",)),
        name="rms_norm",
    )(x, g.reshape(1, d))


def _gelu_tanh(x):
    return 0.5 * x * (1.0 + jnp.tanh(0.7978845608028654 * (x + 0.044715 * (x * x * x))))


def _mm_kernel(*refs, nk, has_res, act):
    if has_res:
        a_ref, w_ref, r_ref, o_ref = refs[:4]
        scratch = refs[4:]
    else:
        a_ref, w_ref, o_ref = refs[:3]
        r_ref = None
        scratch = refs[3:]

    bn = o_ref.shape[1]
    sub = min(bn, MXU_COLS)

    def step(first, last):
        for s in range(bn // sub):
            cs = slice(s * sub, (s + 1) * sub)
            d = jnp.dot(a_ref[...], w_ref[:, cs], preferred_element_type=f32)
            if not first:
                d = scratch[0][:, cs] + d
            if not last:
                scratch[0][:, cs] = d
                continue
            if act == "gelu":
                d = _gelu_tanh(d)
            if has_res:
                d = d + r_ref[:, cs]
            o_ref[:, cs] = d.astype(o_ref.dtype)

    if nk == 1:
        step(True, True)
        return
    k = pl.program_id(2)
    pl.when(k == 0)(functools.partial(step, True, False))
    pl.when(jnp.logical_and(k > 0, k < nk - 1))(functools.partial(step, False, False))
    pl.when(k == nk - 1)(functools.partial(step, False, True))


def _pick_bk(kdim, bm, bn, out_bytes, has_res):
    fixed = bm * bn * (4 + 2 * out_bytes + (8 if has_res else 0))
    for nk in range(1, kdim // LANE + 1):
        bk = kdim // nk
        acc = bm * bn * 4 if nk > 1 else 0
        if kdim % nk == 0 and bk % LANE == 0 and fixed + acc + 4 * bk * (bm + bn) <= MM_VMEM_BUDGET_BYTES:
            return bk
    raise ValueError("no K tile fits")


def matmul(a, w, res=None, act=None, out_dtype=f32, bm=1024, bn=1024):
    m, kdim = a.shape
    _, n = w.shape
    bn = min(bn, n)
    bk = _pick_bk(kdim, bm, bn, jnp.dtype(out_dtype).itemsize, res is not None)
    assert m % bm == 0 and n % bn == 0 and kdim % bk == 0
    nk = kdim // bk
    in_specs = [pl.BlockSpec((bm, bk), lambda j, i, k: (i, k)),
                pl.BlockSpec((bk, bn), lambda j, i, k: (k, j))]
    args = [a, w]
    if res is not None:
        in_specs.append(pl.BlockSpec((bm, bn), lambda j, i, k: (i, j)))
        args.append(res)
    scratch = [pltpu.VMEM((bm, bn), f32)] if nk > 1 else []
    return pl.pallas_call(
        functools.partial(_mm_kernel, nk=nk, has_res=res is not None, act=act),
        grid=(n // bn, m // bm, nk),
        in_specs=in_specs,
        out_specs=pl.BlockSpec((bm, bn), lambda j, i, k: (i, j)),
        out_shape=jax.ShapeDtypeStruct((m, n), out_dtype),
        scratch_shapes=scratch,
        compiler_params=_cparams(("parallel", "parallel", "arbitrary")),
        name="matmul",
    )(*args)


def _mm_ws_kernel(*refs, nw, nt, nm, ck, bn, n_valid, act, has_res, has_side, side_rows):
    refs = list(refs)
    a_ref = refs.pop(0)
    w_refs = [refs.pop(0) for _ in range(nw)]
    r_ref = refs.pop(0) if has_res else None
    s_ref = refs.pop(0) if has_side else None
    o_ref = refs.pop(0)
    so_ref = refs.pop(0) if has_side else None
    wbf_refs = (refs.pop(0), refs.pop(0))
    j = pl.program_id(0)
    i = pl.program_id(1)

    def cast(slot):
        for t in range(nw):
            w = w_refs[t][...]
            if n_valid % bn:
                col = j * bn + lax.broadcasted_iota(jnp.int32, w.shape, 1)
                w = jnp.where(col < n_valid, w, 0.0)
            wbf_refs[slot][t, pl.ds(pl.multiple_of(i * ck, ck), ck), :] = w.astype(bf16)

    def side_cast():
        if has_side:
            rs = s_ref.shape[0]
            slab = jnp.maximum(j - 1, 0) * nm + i * jnp.minimum(j, 1)
            row = slab * rs + lax.broadcasted_iota(jnp.int32, s_ref.shape, 0)
            so_ref[...] = jnp.where(row < side_rows, s_ref[...], 0.0).astype(so_ref.dtype)

    def compute(slot):
        sub = min(bn, MXU_COLS)
        for s in range(bn // sub):
            cs = slice(s * sub, (s + 1) * sub)
            d = jnp.dot(a_ref[...], wbf_refs[slot][0, :, cs], preferred_element_type=f32)
            if nw == 2:
                u = jnp.dot(a_ref[...], wbf_refs[slot][1, :, cs], preferred_element_type=f32)
                d = d * jax.nn.sigmoid(d) * u
            if act == "gelu":
                d = _gelu_tanh(d)
            if has_res:
                d = d + r_ref[:, cs]
            o_ref[:, cs] = d.astype(o_ref.dtype)

    @pl.when(j == 0)
    def _():
        cast(0)
        side_cast()
        o_ref[...] = jnp.zeros(o_ref.shape, o_ref.dtype)

    for parity in (0, 1):
        @pl.when((j > 0) & (j < nt) & (j % 2 == parity))
        def _():
            compute(1 - parity)
            cast(parity)
            side_cast()

    @pl.when(j == nt)
    def _():
        compute((nt - 1) % 2)
        side_cast()


def matmul_ws(a, ws, layer, n_cols, col0=0, res=None, act=None, out_dtype=f32, bm=1024, bn=1024,
              side=None, side_layer=0):
    m, kdim = a.shape
    nw = len(ws)
    nm = m // bm
    ck = kdim // nm
    nt = pl.cdiv(n_cols, bn)
    ct0 = col0 // bn
    assert m % bm == 0 and kdim % nm == 0 and ck % 8 == 0 and col0 % bn == 0
    a_spec = pl.BlockSpec((bm, kdim), lambda j, i: (i * jnp.minimum(j, 1), 0))
    w_spec = pl.BlockSpec((None, ck, bn), lambda j, i: (layer, i, ct0 + jnp.minimum(j, nt - 1)))
    o_map = lambda j, i: (i * jnp.minimum(j, 1), jnp.maximum(j - 1, 0))
    in_specs = [a_spec] + [w_spec] * nw
    args = [a] + list(ws)
    if res is not None:
        in_specs.append(pl.BlockSpec((bm, bn), o_map))
        args.append(res)
    out_specs = [pl.BlockSpec((bm, bn), o_map)]
    out_shape = [jax.ShapeDtypeStruct((m, nt * bn), out_dtype)]
    side_rows = 0
    if side is not None:
        _, side_rows, side_cols = side.shape
        n_slabs = nt * nm
        rs = -(-side_rows // n_slabs)
        rs = -(-rs // 16) * 16
        last = side_rows // rs - (1 if side_rows % rs == 0 else 0)
        s_map = lambda j, i: (jnp.maximum(j - 1, 0) * nm + i * jnp.minimum(j, 1), 0)
        in_specs.append(pl.BlockSpec((None, rs, side_cols),
                                     lambda j, i: (side_layer, jnp.minimum(s_map(j, i)[0], last), 0)))
        args.append(side)
        out_specs.append(pl.BlockSpec((rs, side_cols), s_map))
        out_shape.append(jax.ShapeDtypeStruct((n_slabs * rs, side_cols), bf16))
    outs = pl.pallas_call(
        functools.partial(_mm_ws_kernel, nw=nw, nt=nt, nm=nm, ck=ck, bn=bn, n_valid=n_cols, act=act,
                          has_res=res is not None, has_side=side is not None, side_rows=side_rows),
        grid=(nt + 1, nm),
        in_specs=in_specs,
        out_specs=out_specs,
        out_shape=out_shape,
        scratch_shapes=[pltpu.VMEM((nw, kdim, bn), bf16), pltpu.VMEM((nw, kdim, bn), bf16)],
        compiler_params=_cparams(("arbitrary", "arbitrary")),
        name="matmul_ws",
    )(*args)
    return outs if side is not None else outs[0]


def _rope(x, c, s_lo, s_hi):
    half = ROPE_DIM // 2
    return x * c + pltpu.roll(x, HEAD_DIM - half, 1) * s_lo + pltpu.roll(x, half, 1) * s_hi


def _qkv_post_kernel(z_ref, c_ref, slo_ref, shi_ref,
                     q_ref, kc_ref, vc_ref, ks_ref, vs_ref, kw_ref, vw_ref):
    c, s_lo, s_hi = c_ref[...], slo_ref[...], shi_ref[...]
    for h in range(NSA_HEADS):
        sl = slice(h * HEAD_DIM, (h + 1) * HEAD_DIM)
        q_ref[:, sl] = (_rope(z_ref[:, sl], c, s_lo, s_hi) * ATTN_SCALE).astype(q_ref.dtype)
    base = NSA_WIDTH
    for idx, (o_ref, roped) in enumerate(((kc_ref, True), (vc_ref, False), (ks_ref, True),
                                          (vs_ref, False), (kw_ref, True), (vw_ref, False))):
        for g in range(NSA_KV_HEADS):
            col = base + idx * KV_WIDTH + g * HEAD_DIM
            x = z_ref[:, col:col + HEAD_DIM]
            if roped:
                x = _rope(x, c, s_lo, s_hi)
            o_ref[:, g * HEAD_DIM:(g + 1) * HEAD_DIM] = x.astype(o_ref.dtype)


def _rope_tables(t_len):
    half = ROPE_DIM // 2
    inv_freq = ROPE_THETA ** (-jnp.arange(half, dtype=f32) * 2.0 / ROPE_DIM)
    ang = jnp.arange(t_len).astype(f32)[:, None] * inv_freq[None, :]
    cos, sin = jnp.cos(ang), jnp.sin(ang)
    ones = jnp.ones((t_len, HEAD_DIM - ROPE_DIM), f32)
    zeros = jnp.zeros((t_len, HEAD_DIM - ROPE_DIM), f32)
    zh = jnp.zeros((t_len, half), f32)
    c = jnp.concatenate([cos, cos, ones], axis=1)
    s_lo = jnp.concatenate([-sin, zh, zeros], axis=1)
    s_hi = jnp.concatenate([zh, sin, zeros], axis=1)
    return c, s_lo, s_hi


def qkv_post(z, t_len, tm=256):
    m = z.shape[0]
    c, s_lo, s_hi = _rope_tables(t_len)
    nt = t_len // tm
    tab = pl.BlockSpec((tm, HEAD_DIM), lambda i: (i % nt, 0))
    kv = pl.BlockSpec((tm, KV_WIDTH), lambda i: (i, 0))
    shp = lambda dt: jax.ShapeDtypeStruct((m, KV_WIDTH), dt)
    return pl.pallas_call(
        _qkv_post_kernel,
        grid=(m // tm,),
        in_specs=[pl.BlockSpec((tm, z.shape[1]), lambda i: (i, 0)), tab, tab, tab],
        out_specs=[pl.BlockSpec((tm, NSA_WIDTH), lambda i: (i, 0)), kv, kv, kv, kv, kv, kv],
        out_shape=[jax.ShapeDtypeStruct((m, NSA_WIDTH), bf16),
                   shp(f32), shp(f32), shp(bf16), shp(bf16), shp(bf16), shp(bf16)],
        compiler_params=_cparams(("parallel",)),
        name="qkv_post",
    )(z, c, s_lo, s_hi)


def _compress_kernel(a_ref, pe_ref, w1_ref, w2_ref, o_ref):
    half = CMP_STRIDE * HEAD_DIM
    a = a_ref[0, 0]
    n_chunks = a.shape[0]
    top = jnp.dot((a + pe_ref[:, :half]).astype(bf16), w1_ref[:half, :].astype(bf16),
                  preferred_element_type=f32)
    bot = jnp.dot((a + pe_ref[:, half:]).astype(bf16), w1_ref[half:, :].astype(bf16),
                  preferred_element_type=f32)
    pre = top + pltpu.roll(bot, n_chunks - 1, 0)
    hid = _gelu_tanh(pre)
    o_ref[0, 0] = jnp.dot(hid.astype(bf16), w2_ref[...].astype(bf16),
                          preferred_element_type=f32).astype(o_ref.dtype)


def compress(a, pe, w1, w2):
    b, g, n_chunks, width = a.shape
    return pl.pallas_call(
        _compress_kernel,
        grid=(b, g),
        in_specs=[pl.BlockSpec((1, 1, n_chunks, width), lambda i, j: (i, j, 0, 0)),
                  pl.BlockSpec((1, 2 * width), lambda i, j: (0, 0)),
                  pl.BlockSpec((2 * width, CMP_HIDDEN), lambda i, j: (0, 0)),
                  pl.BlockSpec((CMP_HIDDEN, HEAD_DIM), lambda i, j: (0, 0))],
        out_specs=pl.BlockSpec((1, 1, n_chunks, HEAD_DIM), lambda i, j: (i, j, 0, 0)),
        out_shape=jax.ShapeDtypeStruct((b, g, n_chunks, HEAD_DIM), bf16),
        compiler_params=_cparams(("parallel", "parallel")),
        name="compress",
    )(a, pe.reshape(1, 2 * width), w1.reshape(2 * width, CMP_HIDDEN), w2)


def _nsa_kernel(q_ref, kc_ref, vct_ref, ks_ref, vst_ref, kw_ref, vwt_ref, g_ref, o_ref,
                imp_ref, score_ref, sel_ref):
    tq, ck = ATT_TQ, ATT_CK
    nl = NSA_HPG * tq
    qi = pl.program_id(2)
    t0 = qi * tq
    qt = jnp.concatenate([q_ref[0, h] for h in range(NSA_HPG)], axis=1)
    n_cmp = kc_ref.shape[2]
    n_slc = sel_ref.shape[0]

    kpos0 = lax.broadcasted_iota(jnp.int32, (ck, tq), 0)
    qpos = t0 + lax.broadcasted_iota(jnp.int32, (ck, tq), 1)
    c_last = (t0 + tq - 1) // ck

    def raw_scores(k_ref, c):
        k = k_ref[0, pl.ds(pl.multiple_of(c * ck, ck), ck), :]
        return jnp.dot(k, qt, preferred_element_type=f32)

    def masked(sc, mask):
        return jnp.where(jnp.concatenate([mask] * NSA_HPG, axis=1), sc, NEG)

    s = jnp.dot(kc_ref[0, 0], qt, preferred_element_type=f32)
    n_win = 1 + -(-(WINDOW - 1) // ck)
    c_win = [c_last - back for back in range(n_win)]
    s_win = [raw_scores(kw_ref, jnp.maximum(c, 0)) for c in c_win]

    def pair_scores(c):
        return raw_scores(ks_ref, c), raw_scores(ks_ref, c + 1)

    blk_end = lax.broadcasted_iota(jnp.int32, (n_cmp, nl), 0) * CMP_STRIDE + (CMP_BLOCK - 1)
    qpos4 = t0 + lax.broadcasted_iota(jnp.int32, (n_cmp, nl), 1) % tq
    valid = blk_end <= qpos4
    s = jnp.where(valid, s, NEG)
    mx = jnp.max(s, axis=0, keepdims=True)
    e = jnp.where(valid, jnp.exp(s - mx), 0.0)
    den = jnp.sum(e, axis=0, keepdims=True)
    p = e / jnp.where(den > 0.0, den, 1.0)
    o_cmp = jnp.dot(vct_ref[0, 0], p.astype(bf16), preferred_element_type=f32)
    imp_ref[...] = (p[:, 0:tq] + p[:, tq:2 * tq]) + p[:, 2 * tq:3 * tq] + p[:, 3 * tq:4 * tq]

    for i, c in enumerate(c_win):
        kpos = kpos0 + c * ck
        s_win[i] = masked(s_win[i], (kpos <= qpos) & (kpos > qpos - WINDOW) & (kpos >= 0))
    m_w = functools.reduce(jnp.maximum, [jnp.max(sw, axis=0, keepdims=True) for sw in s_win])
    p_win = [jnp.exp(sw - m_w) for sw in s_win]
    l_w = functools.reduce(jnp.add, [jnp.sum(pw, axis=0, keepdims=True) for pw in p_win])
    acc_w = functools.reduce(jnp.add, [
        jnp.dot(vwt_ref[0, 0, jnp.maximum(c, 0)], pw.astype(bf16), preferred_element_type=f32)
        for c, pw in zip(c_win, p_win)])
    o_win = acc_w / l_w

    r = SLC_BLOCK // CMP_STRIDE
    rows = [imp_ref[pl.ds(k, n_slc, stride=r), :] for k in range(r)]
    blk = lax.broadcasted_iota(jnp.int32, (n_slc, tq), 0)
    prev = jnp.where(blk == 0, 0.0, pltpu.roll(rows[r - 1], 1, 0))
    score = (rows[0] + rows[1] + rows[2]) + 0.5 * rows[r - 1] + 0.5 * prev
    cur = (t0 + lax.broadcasted_iota(jnp.int32, (n_slc, tq), 1)) // SLC_BLOCK
    score = jnp.where((blk == cur) | (blk == 0), -NEG, jnp.where(blk > cur, NEG, score))
    score_ref[...] = score
    cnt = jnp.zeros((n_slc, tq), jnp.int32)
    for jp in range(n_slc):
        row = jnp.broadcast_to(score_ref[pl.ds(jp, 1), :], (n_slc, tq))
        beats = (row > score) | ((row == score) & (blk > jp))
        cnt = cnt + beats.astype(jnp.int32)
    sel_ref[...] = jnp.where(cnt < min(SLC_TOPK, n_slc), 1.0, 0.0)

    init = (jnp.full((1, nl), NEG, f32), jnp.zeros((1, nl), f32), jnp.zeros((HEAD_DIM, nl), f32))

    def flash_update(c, sc, carry):
        m, l, acc = carry
        per_chunk = ck // SLC_BLOCK
        selrows = jnp.concatenate(
            [jnp.broadcast_to(sel_ref[pl.ds(c * per_chunk + i, 1), :], (SLC_BLOCK, tq))
             for i in range(per_chunk)], axis=0)
        sc = masked(sc, (selrows > 0.5) & (kpos0 + c * ck <= qpos))
        m_new = jnp.maximum(m, jnp.max(sc, axis=0, keepdims=True))
        alpha = jnp.exp(m - m_new)
        pr = jnp.exp(sc - m_new)
        l = alpha * l + jnp.sum(pr, axis=0, keepdims=True)
        acc = alpha * acc + jnp.dot(vst_ref[0, 0, c], pr.astype(bf16), preferred_element_type=f32)
        return m_new, l, acc

    def pair_body(i, carry):
        s_a, s_b = pair_scores(2 * i)
        return flash_update(2 * i, s_a, carry[0]), flash_update(2 * i + 1, s_b, carry[1])

    (m_a, l_a, acc_a), (m_b, l_b, acc_b) = lax.fori_loop(0, (c_last + 2) // 2, pair_body, (init, init))
    m_s = jnp.maximum(m_a, m_b)
    w_a, w_b = jnp.exp(m_a - m_s), jnp.exp(m_b - m_s)
    o_slc = (w_a * acc_a + w_b * acc_b) / (w_a * l_a + w_b * l_b)

    gate = jax.nn.sigmoid(g_ref[0, 0])
    for h in range(NSA_HPG):
        sl = slice(h * tq, (h + 1) * tq)
        o_h = (gate[3 * h:3 * h + 1, :] * o_cmp[:, sl] + gate[3 * h + 1:3 * h + 2, :] * o_slc[:, sl]
               + gate[3 * h + 2:3 * h + 3, :] * o_win[:, sl])
        o_ref[0, :, h * HEAD_DIM:(h + 1) * HEAD_DIM] = o_h.T.astype(o_ref.dtype)


def nsa_attention(qt, k_cmp, v_cmp_t, ks, vs_t, kw, vw_t, gates_t):
    b, _, _, t_len = qt.shape
    tq, ck = ATT_TQ, ATT_CK
    n_cmp = k_cmp.shape[2]
    n_slc = t_len // SLC_BLOCK
    assert (t_len // ck) % 2 == 0
    kfull =pl.BlockSpec((1, t_len, HEAD_DIM), lambda i, g, q: (i, 0, g))
    vfull = pl.BlockSpec((1, 1, t_len // ck, HEAD_DIM, ck), lambda i, g, q: (i, g, 0, 0, 0))
    return pl.pallas_call(
        _nsa_kernel,
        grid=(b, NSA_KV_HEADS, t_len // tq),
        in_specs=[pl.BlockSpec((1, NSA_HPG, HEAD_DIM, tq), lambda i, g, q: (i, g, 0, q)),
                  pl.BlockSpec((1, 1, n_cmp, HEAD_DIM), lambda i, g, q: (i, g, 0, 0)),
                  pl.BlockSpec((1, 1, HEAD_DIM, n_cmp), lambda i, g, q: (i, g, 0, 0)),
                  kfull, vfull, kfull, vfull,
                  pl.BlockSpec((1, 1, 3 * NSA_HPG, tq), lambda i, g, q: (i, g, 0, q))],
        out_specs=pl.BlockSpec((1, tq, NSA_HPG * HEAD_DIM), lambda i, g, q: (i, q, g)),
        out_shape=jax.ShapeDtypeStruct((b, t_len, NSA_WIDTH), bf16),
        scratch_shapes=[pltpu.VMEM((n_cmp, tq), f32), pltpu.VMEM((n_slc, tq), f32),
                        pltpu.VMEM((n_slc, tq), f32)],
        compiler_params=_cparams(("parallel", "parallel", "arbitrary")),
        name="nsa_attention",
    )(qt, k_cmp, v_cmp_t, ks, vs_t, kw, vw_t, gates_t)


def _pool_kernel(x_ref, halo_ref, w_ref, scale_ref, o_ref, *, t_len):
    tm = x_ref.shape[0]
    i = pl.program_id(0)
    t_start = (i * tm) % t_len
    halo = jnp.where(t_start == 0, 0.0, halo_ref[...])
    tpos = t_start + lax.broadcasted_iota(jnp.int32, (tm, 1), 0)
    for gi, w_len in enumerate(POOL_WINDOWS):
        sl = slice(gi * POOL_GROUP, (gi + 1) * POOL_GROUP)
        x = x_ref[:, sl]
        acc = jnp.concatenate([halo[:, sl], x], axis=0)
        first = -POOL_HALO
        span = 1
        while span < w_len:
            acc = acc[span:, :] + acc[:-span, :]
            first += span
            span *= 2
        wsum = acc[-first:-first + tm, :]
        count = jnp.minimum(tpos + 1, w_len).astype(f32)
        y = (wsum / count - x).astype(bf16)
        y = jnp.dot(y, w_ref[gi].astype(bf16), preferred_element_type=f32)
        o_ref[:, sl] = (y * scale_ref[:, sl]).astype(o_ref.dtype)


def pool_mixer(x, w, scale, t_len, tm=256):
    m, c = x.shape
    hb = tm // POOL_HALO
    return pl.pallas_call(
        functools.partial(_pool_kernel, t_len=t_len),
        grid=(m // tm,),
        in_specs=[pl.BlockSpec((tm, c), lambda i: (i, 0)),
                  pl.BlockSpec((POOL_HALO, c), lambda i: (jnp.maximum(i * hb - 1, 0), 0)),
                  pl.BlockSpec(w.shape, lambda i: (0, 0, 0)),
                  pl.BlockSpec((1, c), lambda i: (0, 0))],
        out_specs=pl.BlockSpec((tm, c), lambda i: (i, 0)),
        out_shape=jax.ShapeDtypeStruct((m, c), bf16),
        compiler_params=_cparams(("parallel",)),
        name="pool_mixer",
    )(x, x, w, scale.reshape(1, c))


def _gmlp_kernel(u_ref, v_ref, lg_ref, lb_ref, ws_ref, bs_ref, o_ref, vn_ref):
    tc = v_ref.shape[0]
    vf = v_ref[...].astype(f32)
    mu = jnp.mean(vf, axis=-1, keepdims=True)
    d = vf - mu
    var = jnp.mean(d * d, axis=-1, keepdims=True)
    vn_ref[...] = (d * lax.rsqrt(var + EPS) * lg_ref[...] + lb_ref[...]).astype(bf16)
    gd = v_ref.shape[1] // GMLP_GROUPS
    tri = lax.broadcasted_iota(jnp.int32, (tc, tc), 0) >= lax.broadcasted_iota(jnp.int32, (tc, tc), 1)
    for g in range(GMLP_GROUPS):
        sl = slice(g * gd, (g + 1) * gd)
        w = jnp.where(tri, ws_ref[g], 0.0).astype(bf16)
        s = jnp.dot(w, vn_ref[:, sl], preferred_element_type=f32) + bs_ref[:, g:g + 1]
        o_ref[:, sl] = (u_ref[:, sl].astype(f32) * s).astype(o_ref.dtype)


def gmlp_gate(z, ln_g, ln_b, ws, bs):
    m, two_w = z.shape
    width = two_w // 2
    tc = GMLP_CHUNK
    row = pl.BlockSpec((1, width), lambda i: (0, 0))
    return pl.pallas_call(
        _gmlp_kernel,
        grid=(m // tc,),
        in_specs=[pl.BlockSpec((tc, width), lambda i: (i, 0)),
                  pl.BlockSpec((tc, width), lambda i: (i, 1)),
                  row, row,
                  pl.BlockSpec(ws.shape, lambda i: (0, 0, 0)),
                  pl.BlockSpec((tc, GMLP_GROUPS), lambda i: (0, 0))],
        out_specs=pl.BlockSpec((tc, width), lambda i: (i, 0)),
        out_shape=jax.ShapeDtypeStruct((m, width), bf16),
        scratch_shapes=[pltpu.VMEM((tc, width), bf16)],
        compiler_params=_cparams(("parallel",)),
        name="gmlp_gate",
    )(z, z, ln_g.reshape(1, width), ln_b.reshape(1, width), ws, bs.T)


def _ffn(x, norm_g, wg, wu, wd, layer):
    h = rms_norm(x, norm_g, bf16)
    gu, wd_bf = matmul_ws(h, (wg, wu), layer, wg.shape[2], out_dtype=bf16, bn=FFN_PAD,
                          side=wd, side_layer=layer)
    return matmul(gu, wd_bf, res=x)


def _even_layer(x, b, t_len, norm_g, w_in, j, pe_k, pe_v, w1_k, w2_k, w1_v, w2_v, pool_w, pool_scale, w_out):
    m = x.shape[0]
    g_cnt = NSA_KV_HEADS
    h = rms_norm(x, norm_g, bf16)
    qkv_w = NSA_WIDTH + 6 * KV_WIDTH
    w_gate = jnp.pad(w_in[j, :, qkv_w:qkv_w + N_GATES].astype(bf16), ((0, 0), (0, LANE - N_GATES)))
    w_pool = w_in[j:j + 1, :, qkv_w + N_GATES:]
    z = matmul_ws(h, (w_in,), j, qkv_w)
    gates = matmul(h, w_gate, bn=LANE)[:, :N_GATES]
    pin = matmul_ws(h, (w_pool,), 0, w_pool.shape[2])

    q, kc, vc, ks, vs, kw, vw = qkv_post(z, t_len)

    def chunks(a):
        a = a.reshape(b, t_len // CMP_STRIDE, CMP_STRIDE, g_cnt, HEAD_DIM).transpose(0, 3, 1, 2, 4)
        return a.reshape(b, g_cnt, t_len // CMP_STRIDE, CMP_STRIDE * HEAD_DIM)

    k_cmp = compress(chunks(kc), pe_k, w1_k, w2_k)
    v_cmp_t = compress(chunks(vc), pe_v, w1_v, w2_v).transpose(0, 1, 3, 2)

    def v_layout(a):
        a = a.reshape(b, t_len // ATT_CK, ATT_CK, g_cnt, HEAD_DIM)
        return a.transpose(0, 3, 1, 4, 2)

    qt = q.reshape(b, t_len, NSA_HEADS, HEAD_DIM).transpose(0, 2, 3, 1)
    gates_t = gates.reshape(b, t_len, g_cnt, 3 * NSA_HPG).transpose(0, 2, 3, 1)
    o = nsa_attention(qt, k_cmp, v_cmp_t, ks.reshape(b, t_len, KV_WIDTH), v_layout(vs),
                      kw.reshape(b, t_len, KV_WIDTH), v_layout(vw), gates_t)
    y_pool = pool_mixer(pin, pool_w, pool_scale, t_len)
    cat = jnp.concatenate([o.reshape(m, NSA_WIDTH), y_pool], axis=1)
    return matmul_ws(cat, (w_out,), j, w_out.shape[2], res=x, bn=512)


def _odd_layer(x, norm_g, w_in, j, ln_g, ln_b, ws, bs, w_out):
    h = rms_norm(x, norm_g, bf16)
    z, w_out_bf = matmul_ws(h, (w_in,), j, w_in.shape[2], act="gelu", out_dtype=bf16,
                            side=w_out, side_layer=j)
    y = gmlp_gate(z, ln_g, ln_b, ws, bs)
    return matmul(y, w_out_bf, res=x)


def kernel(x, norm_mix_even, w_in_even, cmp_pe_k, cmp_pe_v, cmp_w1_k, cmp_w2_k, cmp_w1_v, cmp_w2_v, pool_w, pool_scale, w_out_even, norm_mix_odd, w_in_odd, gmlp_ln_g, gmlp_ln_b, gmlp_ws, gmlp_bs, w_out_odd, norm_ffn, w_ffn_gate, w_ffn_up, w_ffn_down, norm_final):
    b, t_len, d = x.shape
    depth = norm_ffn.shape[0]
    xf = x.reshape(b * t_len, d)
    for layer in range(depth):
        j = layer // 2
        if layer % 2 == 0:
            xf = _even_layer(xf, b, t_len, norm_mix_even[j], w_in_even, j, cmp_pe_k[j], cmp_pe_v[j],
                             cmp_w1_k[j], cmp_w2_k[j], cmp_w1_v[j], cmp_w2_v[j], pool_w[j],
                             pool_scale[j], w_out_even)
        else:
            xf = _odd_layer(xf, norm_mix_odd[j], w_in_odd, j, gmlp_ln_g[j], gmlp_ln_b[j], gmlp_ws[j],
                            gmlp_bs[j], w_out_odd)
        xf = _ffn(xf, norm_ffn[layer], w_ffn_gate, w_ffn_up, w_ffn_down, layer)
    return rms_norm(xf, norm_final, x.dtype).reshape(b, t_len, d)
```

```python
import functools

import jax
import jax.numpy as jnp
from jax import lax
from jax.experimental import pallas as pl
from jax.experimental.pallas import tpu as pltpu

HEAD_DIM = 128
NSA_HEADS = 16
NSA_KV_HEADS = 4
NSA_HPG = NSA_HEADS // NSA_KV_HEADS
NSA_WIDTH = NSA_HEADS * HEAD_DIM
KV_WIDTH = NSA_KV_HEADS * HEAD_DIM
CMP_STRIDE = 16
CMP_BLOCK = 32
CMP_HIDDEN = 256
SLC_BLOCK = 64
SLC_TOPK = 16
WINDOW = 512
N_GATES = 3 * NSA_HEADS
ATTN_SCALE = HEAD_DIM ** -0.5
ROPE_THETA = 500000.0
ROPE_DIM = HEAD_DIM // 4
POOL_WINDOWS = (2, 4, 8, 16)
POOL_GROUP = 512
POOL_HALO = 16
GMLP_CHUNK = 128
GMLP_GROUPS = 16
EPS = 1e-6
NEG = -1e30

LANE = 128
MXU_COLS = 256
VMEM_LIMIT_BYTES = 56 * 1024 * 1024
ATT_TQ = 128
ATT_CK = 256
FFN_PAD = 512
MM_VMEM_BUDGET_BYTES = 48 * 1024 * 1024

bf16 = jnp.bfloat16
f32 = jnp.float32


def _cparams(sem):
    return pltpu.CompilerParams(dimension_semantics=sem, vmem_limit_bytes=VMEM_LIMIT_BYTES)


def _rms_kernel(x_ref, g_ref, o_ref):
    x = x_ref[...]
    ms = jnp.mean(x * x, axis=-1, keepdims=True)
    o_ref[...] = (x * lax.rsqrt(ms + EPS) * g_ref[...]).astype(o_ref.dtype)


def rms_norm(x, g, out_dtype, tm=256):
    m, d = x.shape
    return pl.pallas_call(
        _rms_kernel,
        grid=(m // tm,),
        in_specs=[pl.BlockSpec((tm, d), lambda i: (i, 0)), pl.BlockSpec((1, d), lambda i: (0, 0))],
        out_specs=pl.BlockSpec((tm, d), lambda i: (i, 0)),
        out_shape=jax.ShapeDtypeStruct((m, d), out_dtype),
        compiler_params=_cparams(("parallel",)),
        name="rms_norm",
    )(x, g.reshape(1, d))


def _gelu_tanh(x):
    c0 = 2.0 * 0.7978845608028654
    return x * jax.nn.sigmoid(x * (c0 + (c0 * 0.044715) * (x * x)))


def _mm_kernel(*refs, nk, has_res, act):
    if has_res:
        a_ref, w_ref, r_ref, o_ref = refs[:4]
        scratch = refs[4:]
    else:
        a_ref, w_ref, o_ref = refs[:3]
        r_ref = None
        scratch = refs[3:]

    bn = o_ref.shape[1]
    sub = min(bn, MXU_COLS)

    def step(first, last):
        for s in range(bn // sub):
            cs = slice(s * sub, (s + 1) * sub)
            d = jnp.dot(a_ref[...], w_ref[:, cs], preferred_element_type=f32)
            if not first:
                d = scratch[0][:, cs] + d
            if not last:
                scratch[0][:, cs] = d
                continue
            if act == "gelu":
                d = _gelu_tanh(d)
            if has_res:
                d = d + r_ref[:, cs]
            o_ref[:, cs] = d.astype(o_ref.dtype)

    if nk == 1:
        step(True, True)
        return
    k = pl.program_id(2)
    pl.when(k == 0)(functools.partial(step, True, False))
    pl.when(jnp.logical_and(k > 0, k < nk - 1))(functools.partial(step, False, False))
    pl.when(k == nk - 1)(functools.partial(step, False, True))


def _pick_bk(kdim, bm, bn, out_bytes, has_res):
    fixed = bm * bn * (4 + 2 * out_bytes + (8 if has_res else 0))
    for nk in range(1, kdim // LANE + 1):
        bk = kdim // nk
        acc = bm * bn * 4 if nk > 1 else 0
        if kdim % nk == 0 and bk % LANE == 0 and fixed + acc + 4 * bk * (bm + bn) <= MM_VMEM_BUDGET_BYTES:
            return bk
    raise ValueError("no K tile fits")


def matmul(a, w, res=None, act=None, out_dtype=f32, bm=1024, bn=1024):
    m, kdim = a.shape
    _, n = w.shape
    bn = min(bn, n)
    bk = _pick_bk(kdim, bm, bn, jnp.dtype(out_dtype).itemsize, res is not None)
    assert m % bm == 0 and n % bn == 0 and kdim % bk == 0
    nk = kdim // bk
    in_specs = [pl.BlockSpec((bm, bk), lambda j, i, k: (i, k)),
                pl.BlockSpec((bk, bn), lambda j, i, k: (k, j))]
    args = [a, w]
    if res is not None:
        in_specs.append(pl.BlockSpec((bm, bn), lambda j, i, k: (i, j)))
        args.append(res)
    scratch = [pltpu.VMEM((bm, bn), f32)] if nk > 1 else []
    return pl.pallas_call(
        functools.partial(_mm_kernel, nk=nk, has_res=res is not None, act=act),
        grid=(n // bn, m // bm, nk),
        in_specs=in_specs,
        out_specs=pl.BlockSpec((bm, bn), lambda j, i, k: (i, j)),
        out_shape=jax.ShapeDtypeStruct((m, n), out_dtype),
        scratch_shapes=scratch,
        compiler_params=_cparams(("parallel", "parallel", "arbitrary")),
        name="matmul",
    )(*args)


def _mm_ws_kernel(*refs, na, nw, nt, nm, ck, bn, n_valid, act, has_res, has_side, side_rows):
    refs = list(refs)
    a_refs = [refs.pop(0) for _ in range(na)]
    w_refs = [refs.pop(0) for _ in range(nw)]
    r_ref = refs.pop(0) if has_res else None
    s_ref = refs.pop(0) if has_side else None
    o_ref = refs.pop(0)
    so_ref = refs.pop(0) if has_side else None
    wbf_ref = refs.pop(0)
    j = pl.program_id(0)
    i = pl.program_id(1)

    @pl.when(j < nt)
    def _():
        for t in range(nw):
            w = w_refs[t][...]
            if n_valid % bn:
                col = j * bn + lax.broadcasted_iota(jnp.int32, w.shape, 1)
                w = jnp.where(col < n_valid, w, 0.0)
            wbf_ref[(j % 2) * nw + t, pl.ds(pl.multiple_of(i * ck, ck), ck), :] = w.astype(bf16)

    @pl.when(j == 0)
    def _():
        o_ref[...] = jnp.zeros(o_ref.shape, o_ref.dtype)

    @pl.when(j > 0)
    def _():
        slot = ((j - 1) % 2) * nw
        sub = min(bn, MXU_COLS)

        def lhs_dot(w_slot, cs):
            k0, acc = 0, None
            for a_ref in a_refs:
                kp = a_ref.shape[1]
                part = jnp.dot(a_ref[...], wbf_ref[w_slot, k0:k0 + kp, cs], preferred_element_type=f32)
                acc = part if acc is None else acc + part
                k0 += kp
            return acc

        for s in range(bn // sub):
            cs = slice(s * sub, (s + 1) * sub)
            d = lhs_dot(slot, cs)
            if nw == 2:
                d = d * jax.nn.sigmoid(d) * lhs_dot(slot + 1, cs)
            if act == "gelu":
                d = _gelu_tanh(d)
            if has_res:
                d = d + r_ref[:, cs]
            o_ref[:, cs] = d.astype(o_ref.dtype)

    if has_side:
        rs = s_ref.shape[0]
        slab = jnp.maximum(j - 1, 0) * nm + i * jnp.minimum(j, 1)
        row = slab * rs + lax.broadcasted_iota(jnp.int32, s_ref.shape, 0)
        so_ref[...] = jnp.where(row < side_rows, s_ref[...], 0.0).astype(so_ref.dtype)


def matmul_ws(a, ws, layer, n_cols, col0=0, res=None, act=None, out_dtype=f32, bm=1024, bn=1024,
              side=None, side_layer=0):
    a_parts = a if isinstance(a, (tuple, list)) else (a,)
    m = a_parts[0].shape[0]
    kdim = sum(p.shape[1] for p in a_parts)
    nw = len(ws)
    nm = m // bm
    ck = kdim // nm
    nt = pl.cdiv(n_cols, bn)
    ct0 = col0 // bn
    assert m % bm == 0 and kdim % nm == 0 and ck % 8 == 0 and col0 % bn == 0
    a_specs = [pl.BlockSpec((bm, p.shape[1]), lambda j, i: (i * jnp.minimum(j, 1), 0)) for p in a_parts]
    w_spec = pl.BlockSpec((None, ck, bn), lambda j, i: (layer, i, ct0 + jnp.minimum(j, nt - 1)))
    o_map = lambda j, i: (i * jnp.minimum(j, 1), jnp.maximum(j - 1, 0))
    in_specs = a_specs + [w_spec] * nw
    args = list(a_parts) + list(ws)
    if res is not None:
        in_specs.append(pl.BlockSpec((bm, bn), o_map))
        args.append(res)
    out_specs = [pl.BlockSpec((bm, bn), o_map)]
    out_shape = [jax.ShapeDtypeStruct((m, nt * bn), out_dtype)]
    side_rows = 0
    if side is not None:
        _, side_rows, side_cols = side.shape
        n_slabs = nt * nm
        rs = -(-side_rows // n_slabs)
        rs = -(-rs // 16) * 16
        last = side_rows // rs - (1 if side_rows % rs == 0 else 0)
        s_map = lambda j, i: (jnp.maximum(j - 1, 0) * nm + i * jnp.minimum(j, 1), 0)
        in_specs.append(pl.BlockSpec((None, rs, side_cols),
                                     lambda j, i: (side_layer, jnp.minimum(s_map(j, i)[0], last), 0)))
        args.append(side)
        out_specs.append(pl.BlockSpec((rs, side_cols), s_map))
        out_shape.append(jax.ShapeDtypeStruct((n_slabs * rs, side_cols), bf16))
    outs = pl.pallas_call(
        functools.partial(_mm_ws_kernel, na=len(a_parts), nw=nw, nt=nt, nm=nm, ck=ck, bn=bn,
                          n_valid=n_cols, act=act,
                          has_res=res is not None, has_side=side is not None, side_rows=side_rows),
        grid=(nt + 1, nm),
        in_specs=in_specs,
        out_specs=out_specs,
        out_shape=out_shape,
        scratch_shapes=[pltpu.VMEM((2 * nw, kdim, bn), bf16)],
        compiler_params=_cparams(("arbitrary", "arbitrary")),
        name="matmul_ws",
    )(*args)
    return outs if side is not None else outs[0]


def _rope(x, c, s_lo, s_hi):
    half = ROPE_DIM // 2
    return x * c + pltpu.roll(x, HEAD_DIM - half, 1) * s_lo + pltpu.roll(x, half, 1) * s_hi


def _qkv_post_kernel(z_ref, c_ref, slo_ref, shi_ref,
                     q_ref, kc_ref, vc_ref, ks_ref, vs_ref, kw_ref, vw_ref):
    c, s_lo, s_hi = c_ref[...], slo_ref[...], shi_ref[...]
    for h in range(NSA_HEADS):
        sl = slice(h * HEAD_DIM, (h + 1) * HEAD_DIM)
        q_ref[:, sl] = (_rope(z_ref[:, sl], c, s_lo, s_hi) * ATTN_SCALE).astype(q_ref.dtype)
    base = NSA_WIDTH
    for idx, (o_ref, roped) in enumerate(((kc_ref, True), (vc_ref, False), (ks_ref, True),
                                          (vs_ref, False), (kw_ref, True), (vw_ref, False))):
        for g in range(NSA_KV_HEADS):
            col = base + idx * KV_WIDTH + g * HEAD_DIM
            x = z_ref[:, col:col + HEAD_DIM]
            if roped:
                x = _rope(x, c, s_lo, s_hi)
            o_ref[:, g * HEAD_DIM:(g + 1) * HEAD_DIM] = x.astype(o_ref.dtype)


def _rope_tables(t_len):
    half = ROPE_DIM // 2
    inv_freq = ROPE_THETA ** (-jnp.arange(half, dtype=f32) * 2.0 / ROPE_DIM)
    ang = jnp.arange(t_len).astype(f32)[:, None] * inv_freq[None, :]
    cos, sin = jnp.cos(ang), jnp.sin(ang)
    ones = jnp.ones((t_len, HEAD_DIM - ROPE_DIM), f32)
    zeros = jnp.zeros((t_len, HEAD_DIM - ROPE_DIM), f32)
    zh = jnp.zeros((t_len, half), f32)
    c = jnp.concatenate([cos, cos, ones], axis=1)
    s_lo = jnp.concatenate([-sin, zh, zeros], axis=1)
    s_hi = jnp.concatenate([zh, sin, zeros], axis=1)
    return c, s_lo, s_hi


def qkv_post(z, t_len, tm=256):
    m = z.shape[0]
    c, s_lo, s_hi = _rope_tables(t_len)
    nt = t_len // tm
    tab = pl.BlockSpec((tm, HEAD_DIM), lambda i: (i % nt, 0))
    kv = pl.BlockSpec((tm, KV_WIDTH), lambda i: (i, 0))
    shp = lambda dt: jax.ShapeDtypeStruct((m, KV_WIDTH), dt)
    return pl.pallas_call(
        _qkv_post_kernel,
        grid=(m // tm,),
        in_specs=[pl.BlockSpec((tm, z.shape[1]), lambda i: (i, 0)), tab, tab, tab],
        out_specs=[pl.BlockSpec((tm, NSA_WIDTH), lambda i: (i, 0)), kv, kv, kv, kv, kv, kv],
        out_shape=[jax.ShapeDtypeStruct((m, NSA_WIDTH), bf16),
                   shp(f32), shp(f32), shp(bf16), shp(bf16), shp(bf16), shp(bf16)],
        compiler_params=_cparams(("parallel",)),
        name="qkv_post",
    )(z, c, s_lo, s_hi)


def _compress_kernel(a_ref, pe_ref, w1_ref, w2_ref, o_ref):
    n_chunks = a_ref.shape[1] // CMP_STRIDE
    top = bot = None
    for l in range(CMP_STRIDE):
        rows = a_ref[0, pl.ds(l, n_chunks, stride=CMP_STRIDE), :]
        t = jnp.dot((rows + pe_ref[l:l + 1, :]).astype(bf16), w1_ref[l].astype(bf16),
                    preferred_element_type=f32)
        u = jnp.dot((rows + pe_ref[CMP_STRIDE + l:CMP_STRIDE + l + 1, :]).astype(bf16),
                    w1_ref[CMP_STRIDE + l].astype(bf16), preferred_element_type=f32)
        top = t if top is None else top + t
        bot = u if bot is None else bot + u
    pre = top + pltpu.roll(bot, n_chunks - 1, 0)
    hid = _gelu_tanh(pre)
    o_ref[0, 0] = jnp.dot(hid.astype(bf16), w2_ref[...].astype(bf16),
                          preferred_element_type=f32).astype(o_ref.dtype)


def compress(a, pe, w1, w2):
    b, t_len, width = a.shape
    n_chunks = t_len // CMP_STRIDE
    return pl.pallas_call(
        _compress_kernel,
        grid=(b, width // HEAD_DIM),
        in_specs=[pl.BlockSpec((1, t_len, HEAD_DIM), lambda i, j: (i, 0, j)),
                  pl.BlockSpec(pe.shape, lambda i, j: (0, 0)),
                  pl.BlockSpec(w1.shape, lambda i, j: (0, 0, 0)),
                  pl.BlockSpec(w2.shape, lambda i, j: (0, 0))],
        out_specs=pl.BlockSpec((1, 1, n_chunks, HEAD_DIM), lambda i, j: (i, j, 0, 0)),
        out_shape=jax.ShapeDtypeStruct((b, width // HEAD_DIM, n_chunks, HEAD_DIM), bf16),
        compiler_params=_cparams(("parallel", "parallel")),
        name="compress",
    )(a, pe, w1, w2)


def _nsa_kernel(q_ref, kc_ref, vc_ref, ks_ref, vs_ref, kw_ref, vw_ref, g_ref, o_ref,
                imp_ref, score_ref, sel_ref, gate_ref):
    tq, ck = ATT_TQ, ATT_CK
    nl = NSA_HPG * tq
    grp = pl.program_id(1)
    qi = pl.program_id(2)
    t0 = qi * tq
    qt = jnp.concatenate([q_ref[0, :, h * HEAD_DIM:(h + 1) * HEAD_DIM].T for h in range(NSA_HPG)],
                         axis=1)
    n_cmp = kc_ref.shape[2]
    n_slc = sel_ref.shape[0]

    def pv(v, pr):
        return lax.dot_general(v, pr.astype(bf16), (((0,), (0,)), ((), ())), preferred_element_type=f32)

    def v_chunk(v_ref, c):
        return v_ref[0, pl.ds(pl.multiple_of(c * ck, ck), ck), :]

    kpos0 = lax.broadcasted_iota(jnp.int32, (ck, tq), 0)
    qpos = t0 + lax.broadcasted_iota(jnp.int32, (ck, tq), 1)
    c_last = (t0 + tq - 1) // ck

    def raw_scores(k_ref, c):
        k = k_ref[0, pl.ds(pl.multiple_of(c * ck, ck), ck), :]
        return jnp.dot(k, qt, preferred_element_type=f32)

    def masked(sc, mask):
        return jnp.where(jnp.concatenate([mask] * NSA_HPG, axis=1), sc, NEG)

    s = jnp.dot(kc_ref[0, 0], qt, preferred_element_type=f32)
    n_win = 1 + -(-(WINDOW - 1) // ck)
    c_win = [c_last - back for back in range(n_win)]
    s_win = [raw_scores(kw_ref, jnp.maximum(c, 0)) for c in c_win]

    def pair_scores(c):
        return raw_scores(ks_ref, c), raw_scores(ks_ref, c + 1)

    blk_end = lax.broadcasted_iota(jnp.int32, (n_cmp, nl), 0) * CMP_STRIDE + (CMP_BLOCK - 1)
    qpos4 = t0 + lax.broadcasted_iota(jnp.int32, (n_cmp, nl), 1) % tq
    valid = blk_end <= qpos4
    s = jnp.where(valid, s, NEG)
    mx = jnp.max(s, axis=0, keepdims=True)
    e = jnp.where(valid, jnp.exp(s - mx), 0.0)
    den = jnp.sum(e, axis=0, keepdims=True)
    p = e / jnp.where(den > 0.0, den, 1.0)
    o_cmp = pv(vc_ref[0, 0], p)
    imp_ref[...] = (p[:, 0:tq] + p[:, tq:2 * tq]) + p[:, 2 * tq:3 * tq] + p[:, 3 * tq:4 * tq]

    for i, c in enumerate(c_win):
        kpos = kpos0 + c * ck
        s_win[i] = masked(s_win[i], (kpos <= qpos) & (kpos > qpos - WINDOW) & (kpos >= 0))
    m_w = functools.reduce(jnp.maximum, [jnp.max(sw, axis=0, keepdims=True) for sw in s_win])
    p_win = [jnp.exp(sw - m_w) for sw in s_win]
    l_w = functools.reduce(jnp.add, [jnp.sum(pw, axis=0, keepdims=True) for pw in p_win])
    acc_w = functools.reduce(jnp.add, [pv(v_chunk(vw_ref, jnp.maximum(c, 0)), pw)
                                       for c, pw in zip(c_win, p_win)])
    o_win = acc_w / l_w

    r = SLC_BLOCK // CMP_STRIDE
    rows = [imp_ref[pl.ds(k, n_slc, stride=r), :] for k in range(r)]
    blk = lax.broadcasted_iota(jnp.int32, (n_slc, tq), 0)
    prev = jnp.where(blk == 0, 0.0, pltpu.roll(rows[r - 1], 1, 0))
    score = (rows[0] + rows[1] + rows[2]) + 0.5 * rows[r - 1] + 0.5 * prev
    cur = (t0 + lax.broadcasted_iota(jnp.int32, (n_slc, tq), 1)) // SLC_BLOCK
    score = jnp.where((blk == cur) | (blk == 0), -NEG, jnp.where(blk > cur, NEG, score))
    score_ref[...] = score
    cnt = jnp.zeros((n_slc, tq), jnp.int32)
    for jp in range(n_slc):
        row = jnp.broadcast_to(score_ref[pl.ds(jp, 1), :], (n_slc, tq))
        beats = (row > score) | ((row == score) & (blk > jp))
        cnt = cnt + beats.astype(jnp.int32)
    sel_ref[...] = jnp.where(cnt < min(SLC_TOPK, n_slc), 1.0, 0.0)

    init = (jnp.full((1, nl), NEG, f32), jnp.zeros((1, nl), f32), jnp.zeros((HEAD_DIM, nl), f32))

    def flash_update(c, sc, carry):
        m, l, acc = carry
        per_chunk = ck // SLC_BLOCK
        selrows = jnp.concatenate(
            [jnp.broadcast_to(sel_ref[pl.ds(c * per_chunk + i, 1), :], (SLC_BLOCK, tq))
             for i in range(per_chunk)], axis=0)
        sc = masked(sc, (selrows > 0.5) & (kpos0 + c * ck <= qpos))
        m_new = jnp.maximum(m, jnp.max(sc, axis=0, keepdims=True))
        alpha = jnp.exp(m - m_new)
        pr = jnp.exp(sc - m_new)
        l = alpha * l + jnp.sum(pr, axis=0, keepdims=True)
        acc = alpha * acc + pv(v_chunk(vs_ref, c), pr)
        return m_new, l, acc

    def pair_body(i, carry):
        s_a, s_b = pair_scores(2 * i)
        return flash_update(2 * i, s_a, carry[0]), flash_update(2 * i + 1, s_b, carry[1])

    (m_a, l_a, acc_a), (m_b, l_b, acc_b) = lax.fori_loop(0, (c_last + 2) // 2, pair_body, (init, init))
    m_s = jnp.maximum(m_a, m_b)
    w_a, w_b = jnp.exp(m_a - m_s), jnp.exp(m_b - m_s)
    o_slc = (w_a * acc_a + w_b * acc_b) / (w_a * l_a + w_b * l_b)

    gate_ref[...] = jax.nn.sigmoid(g_ref[...].T)
    for h in range(NSA_HPG):
        sl = slice(h * tq, (h + 1) * tq)
        g_c, g_s, g_w = [gate_ref[pl.ds((grp * NSA_HPG + h) * 3 + br, 1), :] for br in range(3)]
        o_h = g_c * o_cmp[:, sl] + g_s * o_slc[:, sl] + g_w * o_win[:, sl]
        o_ref[0, :, h * HEAD_DIM:(h + 1) * HEAD_DIM] = o_h.T.astype(o_ref.dtype)


def nsa_attention(q, k_cmp, v_cmp, ks, vs, kw, vw, gates, t_len):
    b = q.shape[0] // t_len
    tq, ck = ATT_TQ, ATT_CK
    nq = t_len // tq
    n_cmp = k_cmp.shape[2]
    n_slc = t_len // SLC_BLOCK
    assert (t_len // ck) % 2 == 0
    seq = lambda a: a.reshape(b, t_len, a.shape[1])
    kv_full = pl.BlockSpec((1, t_len, HEAD_DIM), lambda i, g, j: (i, 0, g))
    cmp_full = pl.BlockSpec((1, 1, n_cmp, HEAD_DIM), lambda i, g, j: (i, g, 0, 0))
    return pl.pallas_call(
        _nsa_kernel,
        grid=(b, NSA_KV_HEADS, nq),
        in_specs=[pl.BlockSpec((1, tq, NSA_HPG * HEAD_DIM), lambda i, g, j: (i, j, g)),
                  cmp_full, cmp_full, kv_full, kv_full, kv_full, kv_full,
                  pl.BlockSpec((tq, LANE), lambda i, g, j: (i * nq + j, 0))],
        out_specs=pl.BlockSpec((1, tq, NSA_HPG * HEAD_DIM), lambda i, g, j: (i, j, g)),
        out_shape=jax.ShapeDtypeStruct((b, t_len, NSA_WIDTH), bf16),
        scratch_shapes=[pltpu.VMEM((n_cmp, tq), f32), pltpu.VMEM((n_slc, tq), f32),
                        pltpu.VMEM((n_slc, tq), f32), pltpu.VMEM((LANE, tq), f32)],
        compiler_params=_cparams(("parallel", "parallel", "arbitrary")),
        name="nsa_attention",
    )(seq(q), k_cmp, v_cmp, seq(ks), seq(vs), seq(kw), seq(vw), gates)


def _pool_kernel(x_ref, halo_ref, w_ref, scale_ref, o_ref, *, t_len):
    tm = x_ref.shape[0]
    i = pl.program_id(0)
    t_start = (i * tm) % t_len
    halo = jnp.where(t_start == 0, 0.0, halo_ref[...])
    tpos = t_start + lax.broadcasted_iota(jnp.int32, (tm, 1), 0)
    kw = w_ref.shape[1]
    for gi, w_len in enumerate(POOL_WINDOWS):
        sl = slice(gi * POOL_GROUP, (gi + 1) * POOL_GROUP)
        xs = slice(gi * POOL_GROUP, gi * POOL_GROUP + kw)
        x = x_ref[:, xs]
        acc = jnp.concatenate([halo[:, xs], x], axis=0)
        first = -POOL_HALO
        span = 1
        while span < w_len:
            acc = acc[span:, :] + acc[:-span, :]
            first += span
            span *= 2
        wsum = acc[-first:-first + tm, :]
        count = jnp.minimum(tpos + 1, w_len).astype(f32)
        y = (wsum / count - x).astype(bf16)
        y = jnp.dot(y, w_ref[gi].astype(bf16), preferred_element_type=f32)
        o_ref[:, sl] = (y * scale_ref[:, sl]).astype(o_ref.dtype)


def pool_mixer(x, w, scale, t_len, lane0=0, tm=256):
    m = x.shape[0]
    groups, cg, _ = w.shape
    c = groups * cg
    kw = -(-(lane0 + cg) // LANE) * LANE
    xc = (groups - 1) * cg + kw
    assert lane0 < LANE and xc <= x.shape[1]
    w_shift = jnp.pad(w, ((0, 0), (lane0, kw - lane0 - cg), (0, 0)))
    hb = tm // POOL_HALO
    return pl.pallas_call(
        functools.partial(_pool_kernel, t_len=t_len),
        grid=(m // tm,),
        in_specs=[pl.BlockSpec((tm, xc), lambda i: (i, 0)),
                  pl.BlockSpec((POOL_HALO, xc), lambda i: (jnp.maximum(i * hb - 1, 0), 0)),
                  pl.BlockSpec(w_shift.shape, lambda i: (0, 0, 0)),
                  pl.BlockSpec((1, c), lambda i: (0, 0))],
        out_specs=pl.BlockSpec((tm, c), lambda i: (i, 0)),
        out_shape=jax.ShapeDtypeStruct((m, c), bf16),
        compiler_params=_cparams(("parallel",)),
        name="pool_mixer",
    )(x, x, w_shift, scale.reshape(1, c))


def _gmlp_kernel(u_ref, v_ref, lg_ref, lb_ref, ws_ref, bs_ref, o_ref, vn_ref):
    tc = v_ref.shape[0]
    vf = v_ref[...].astype(f32)
    mu = jnp.mean(vf, axis=-1, keepdims=True)
    d = vf - mu
    var = jnp.mean(d * d, axis=-1, keepdims=True)
    vn_ref[...] = (d * lax.rsqrt(var + EPS) * lg_ref[...] + lb_ref[...]).astype(bf16)
    gd = v_ref.shape[1] // GMLP_GROUPS
    tri = lax.broadcasted_iota(jnp.int32, (tc, tc), 0) >= lax.broadcasted_iota(jnp.int32, (tc, tc), 1)
    for g in range(GMLP_GROUPS):
        sl = slice(g * gd, (g + 1) * gd)
        w = jnp.where(tri, ws_ref[g], 0.0).astype(bf16)
        s = jnp.dot(w, vn_ref[:, sl], preferred_element_type=f32) + bs_ref[:, g:g + 1]
        o_ref[:, sl] = (u_ref[:, sl].astype(f32) * s).astype(o_ref.dtype)


def gmlp_gate(z, ln_g, ln_b, ws, bs):
    m, two_w = z.shape
    width = two_w // 2
    tc = GMLP_CHUNK
    row = pl.BlockSpec((1, width), lambda i: (0, 0))
    return pl.pallas_call(
        _gmlp_kernel,
        grid=(m // tc,),
        in_specs=[pl.BlockSpec((tc, width), lambda i: (i, 0)),
                  pl.BlockSpec((tc, width), lambda i: (i, 1)),
                  row, row,
                  pl.BlockSpec(ws.shape, lambda i: (0, 0, 0)),
                  pl.BlockSpec((tc, GMLP_GROUPS), lambda i: (0, 0))],
        out_specs=pl.BlockSpec((tc, width), lambda i: (i, 0)),
        out_shape=jax.ShapeDtypeStruct((m, width), bf16),
        scratch_shapes=[pltpu.VMEM((tc, width), bf16)],
        compiler_params=_cparams(("parallel",)),
        name="gmlp_gate",
    )(z, z, ln_g.reshape(1, width), ln_b.reshape(1, width), ws, bs.T)


def _ffn(x, norm_g, wg, wu, wd, layer):
    h = rms_norm(x, norm_g, bf16)
    gu, wd_bf = matmul_ws(h, (wg, wu), layer, wg.shape[2], out_dtype=bf16, bn=FFN_PAD,
                          side=wd, side_layer=layer)
    return matmul(gu, wd_bf, res=x)


def _even_layer(x, b, t_len, norm_g, w_in, j, pe_k, pe_v, w1_k, w2_k, w1_v, w2_v, pool_w, pool_scale, w_out):
    m = x.shape[0]
    h = rms_norm(x, norm_g, bf16)
    qkv_w = NSA_WIDTH + 6 * KV_WIDTH
    z = matmul_ws(h, (w_in,), j, qkv_w)
    zgp = matmul_ws(h, (w_in,), j, w_in.shape[2] - qkv_w, col0=qkv_w, bn=512)

    q, kc, vc, ks, vs, kw, vw = qkv_post(z, t_len)
    seq = lambda a: a.reshape(b, t_len, a.shape[1])
    k_cmp = compress(seq(kc), pe_k, w1_k, w2_k)
    v_cmp = compress(seq(vc), pe_v, w1_v, w2_v)
    o = nsa_attention(q, k_cmp, v_cmp, ks, vs, kw, vw, zgp, t_len)
    y_pool = pool_mixer(zgp, pool_w, pool_scale, t_len, lane0=N_GATES)
    return matmul_ws((o.reshape(m, NSA_WIDTH), y_pool), (w_out,), j, w_out.shape[2], res=x, bn=512)


def _odd_layer(x, norm_g, w_in, j, ln_g, ln_b, ws, bs, w_out):
    h = rms_norm(x, norm_g, bf16)
    z, w_out_bf = matmul_ws(h, (w_in,), j, w_in.shape[2], act="gelu", out_dtype=bf16,
                            side=w_out, side_layer=j)
    y = gmlp_gate(z, ln_g, ln_b, ws, bs)
    return matmul(y, w_out_bf, res=x)


def kernel(x, norm_mix_even, w_in_even, cmp_pe_k, cmp_pe_v, cmp_w1_k, cmp_w2_k, cmp_w1_v, cmp_w2_v, pool_w, pool_scale, w_out_even, norm_mix_odd, w_in_odd, gmlp_ln_g, gmlp_ln_b, gmlp_ws, gmlp_bs, w_out_odd, norm_ffn, w_ffn_gate, w_ffn_up, w_ffn_down, norm_final):
    b, t_len, d = x.shape
    depth = norm_ffn.shape[0]
    xf = x.reshape(b * t_len, d)
    for layer in range(depth):
        j = layer // 2
        if layer % 2 == 0:
            xf = _even_layer(xf, b, t_len, norm_mix_even[j], w_in_even, j, cmp_pe_k[j], cmp_pe_v[j],
                             cmp_w1_k[j], cmp_w2_k[j], cmp_w1_v[j], cmp_w2_v[j], pool_w[j],
                             pool_scale[j], w_out_even)
        else:
            xf = _odd_layer(xf, norm_mix_odd[j], w_in_odd, j, gmlp_ln_g[j], gmlp_ln_b[j], gmlp_ws[j],
                            gmlp_bs[j], w_out_odd)
        xf = _ffn(xf, norm_ffn[layer], w_ffn_gate, w_ffn_up, w_ffn_down, layer)
    return rms_norm(xf, norm_final, x.dtype).reshape(b, t_len, d)
```

```python
import functools

import jax
import jax.numpy as jnp
from jax import lax
from jax.experimental import pallas as pl
from jax.experimental.pallas import tpu as pltpu

HEAD_DIM = 128
NSA_HEADS = 16
NSA_KV_HEADS = 4
NSA_HPG = NSA_HEADS // NSA_KV_HEADS
NSA_WIDTH = NSA_HEADS * HEAD_DIM
KV_WIDTH = NSA_KV_HEADS * HEAD_DIM
CMP_STRIDE = 16
CMP_BLOCK = 32
CMP_HIDDEN = 256
SLC_BLOCK = 64
SLC_TOPK = 16
WINDOW = 512
N_GATES = 3 * NSA_HEADS
ATTN_SCALE = HEAD_DIM ** -0.5
ROPE_THETA = 500000.0
ROPE_DIM = HEAD_DIM // 4
POOL_WINDOWS = (2, 4, 8, 16)
POOL_GROUP = 512
POOL_HALO = 16
GMLP_CHUNK = 128
GMLP_GROUPS = 16
EPS = 1e-6
NEG = -1e30

LANE = 128
MXU_COLS = 256
VMEM_LIMIT_BYTES = 56 * 1024 * 1024
ATT_TQ = 256
ATT_CK = 256
FFN_PAD = 512
MM_VMEM_BUDGET_BYTES = 48 * 1024 * 1024

bf16 = jnp.bfloat16
f32 = jnp.float32


def _cparams(sem):
    return pltpu.CompilerParams(dimension_semantics=sem, vmem_limit_bytes=VMEM_LIMIT_BYTES)


def _rms_kernel(x_ref, g_ref, o_ref):
    x = x_ref[...]
    ms = jnp.mean(x * x, axis=-1, keepdims=True)
    o_ref[...] = (x * lax.rsqrt(ms + EPS) * g_ref[...]).astype(o_ref.dtype)


def rms_norm(x, g, out_dtype, tm=256):
    m, d = x.shape
    return pl.pallas_call(
        _rms_kernel,
        grid=(m // tm,),
        in_specs=[pl.BlockSpec((tm, d), lambda i: (i, 0)), pl.BlockSpec((1, d), lambda i: (0, 0))],
        out_specs=pl.BlockSpec((tm, d), lambda i: (i, 0)),
        out_shape=jax.ShapeDtypeStruct((m, d), out_dtype),
        compiler_params=_cparams(("parallel",)),
        name="rms_norm",
    )(x, g.reshape(1, d))


def _gelu_tanh(x):
    c0 = 2.0 * 0.7978845608028654
    return x * jax.nn.sigmoid(x * (c0 + (c0 * 0.044715) * (x * x)))


def _mm_kernel(*refs, nk, has_res, act):
    if has_res:
        a_ref, w_ref, r_ref, o_ref = refs[:4]
        scratch = refs[4:]
    else:
        a_ref, w_ref, o_ref = refs[:3]
        r_ref = None
        scratch = refs[3:]

    bn = o_ref.shape[1]
    sub = min(bn, MXU_COLS)

    def step(first, last):
        for s in range(bn // sub):
            cs = slice(s * sub, (s + 1) * sub)
            d = jnp.dot(a_ref[...], w_ref[:, cs], preferred_element_type=f32)
            if not first:
                d = scratch[0][:, cs] + d
            if not last:
                scratch[0][:, cs] = d
                continue
            if act == "gelu":
                d = _gelu_tanh(d)
            if has_res:
                d = d + r_ref[:, cs]
            o_ref[:, cs] = d.astype(o_ref.dtype)

    if nk == 1:
        step(True, True)
        return
    k = pl.program_id(2)
    pl.when(k == 0)(functools.partial(step, True, False))
    pl.when(jnp.logical_and(k > 0, k < nk - 1))(functools.partial(step, False, False))
    pl.when(k == nk - 1)(functools.partial(step, False, True))


def _pick_bk(kdim, bm, bn, out_bytes, has_res):
    fixed = bm * bn * (4 + 2 * out_bytes + (8 if has_res else 0))
    for nk in range(1, kdim // LANE + 1):
        bk = kdim // nk
        acc = bm * bn * 4 if nk > 1 else 0
        if kdim % nk == 0 and bk % LANE == 0 and fixed + acc + 4 * bk * (bm + bn) <= MM_VMEM_BUDGET_BYTES:
            return bk
    raise ValueError("no K tile fits")


def matmul(a, w, res=None, act=None, out_dtype=f32, bm=1024, bn=1024):
    m, kdim = a.shape
    _, n = w.shape
    bn = min(bn, n)
    bk = _pick_bk(kdim, bm, bn, jnp.dtype(out_dtype).itemsize, res is not None)
    assert m % bm == 0 and n % bn == 0 and kdim % bk == 0
    nk = kdim // bk
    in_specs = [pl.BlockSpec((bm, bk), lambda j, i, k: (i, k)),
                pl.BlockSpec((bk, bn), lambda j, i, k: (k, j))]
    args = [a, w]
    if res is not None:
        in_specs.append(pl.BlockSpec((bm, bn), lambda j, i, k: (i, j)))
        args.append(res)
    scratch = [pltpu.VMEM((bm, bn), f32)] if nk > 1 else []
    return pl.pallas_call(
        functools.partial(_mm_kernel, nk=nk, has_res=res is not None, act=act),
        grid=(n // bn, m // bm, nk),
        in_specs=in_specs,
        out_specs=pl.BlockSpec((bm, bn), lambda j, i, k: (i, j)),
        out_shape=jax.ShapeDtypeStruct((m, n), out_dtype),
        scratch_shapes=scratch,
        compiler_params=_cparams(("parallel", "parallel", "arbitrary")),
        name="matmul",
    )(*args)


def _mm_ws_kernel(*refs, na, nw, nt, nm, ck, bn, n_valid, act, has_res, has_side, side_rows):
    refs = list(refs)
    a_refs = [refs.pop(0) for _ in range(na)]
    w_refs = [refs.pop(0) for _ in range(nw)]
    r_ref = refs.pop(0) if has_res else None
    s_ref = refs.pop(0) if has_side else None
    o_ref = refs.pop(0)
    so_ref = refs.pop(0) if has_side else None
    wbf_ref = refs.pop(0)
    j = pl.program_id(0)
    i = pl.program_id(1)

    @pl.when(j < nt)
    def _():
        for t in range(nw):
            w = w_refs[t][...]
            if n_valid % bn:
                col = j * bn + lax.broadcasted_iota(jnp.int32, w.shape, 1)
                w = jnp.where(col < n_valid, w, 0.0)
            wbf_ref[(j % 2) * nw + t, pl.ds(pl.multiple_of(i * ck, ck), ck), :] = w.astype(bf16)

    @pl.when(j == 0)
    def _():
        o_ref[...] = jnp.zeros(o_ref.shape, o_ref.dtype)

    @pl.when(j > 0)
    def _():
        slot = ((j - 1) % 2) * nw
        sub = min(bn, MXU_COLS)

        def lhs_dot(w_slot, cs):
            k0, acc = 0, None
            for a_ref in a_refs:
                kp = a_ref.shape[1]
                part = jnp.dot(a_ref[...], wbf_ref[w_slot, k0:k0 + kp, cs], preferred_element_type=f32)
                acc = part if acc is None else acc + part
                k0 += kp
            return acc

        for s in range(bn // sub):
            cs = slice(s * sub, (s + 1) * sub)
            d = lhs_dot(slot, cs)
            if nw == 2:
                d = d * jax.nn.sigmoid(d) * lhs_dot(slot + 1, cs)
            if act == "gelu":
                d = _gelu_tanh(d)
            if has_res:
                d = d + r_ref[:, cs]
            o_ref[:, cs] = d.astype(o_ref.dtype)

    if has_side:
        rs = s_ref.shape[0]
        slab = jnp.maximum(j - 1, 0) * nm + i * jnp.minimum(j, 1)
        row = slab * rs + lax.broadcasted_iota(jnp.int32, s_ref.shape, 0)
        so_ref[...] = jnp.where(row < side_rows, s_ref[...], 0.0).astype(so_ref.dtype)


def matmul_ws(a, ws, layer, n_cols, col0=0, res=None, act=None, out_dtype=f32, bm=1024, bn=1024,
              side=None, side_layer=0):
    a_parts = a if isinstance(a, (tuple, list)) else (a,)
    m = a_parts[0].shape[0]
    kdim = sum(p.shape[1] for p in a_parts)
    nw = len(ws)
    nm = m // bm
    ck = kdim // nm
    nt = pl.cdiv(n_cols, bn)
    ct0 = col0 // bn
    assert m % bm == 0 and kdim % nm == 0 and ck % 8 == 0 and col0 % bn == 0
    a_specs = [pl.BlockSpec((bm, p.shape[1]), lambda j, i: (i * jnp.minimum(j, 1), 0)) for p in a_parts]
    w_spec = pl.BlockSpec((None, ck, bn), lambda j, i: (layer, i, ct0 + jnp.minimum(j, nt - 1)))
    o_map = lambda j, i: (i * jnp.minimum(j, 1), jnp.maximum(j - 1, 0))
    in_specs = a_specs + [w_spec] * nw
    args = list(a_parts) + list(ws)
    if res is not None:
        in_specs.append(pl.BlockSpec((bm, bn), o_map))
        args.append(res)
    out_specs = [pl.BlockSpec((bm, bn), o_map)]
    out_shape = [jax.ShapeDtypeStruct((m, nt * bn), out_dtype)]
    side_rows = 0
    if side is not None:
        _, side_rows, side_cols = side.shape
        n_slabs = nt * nm
        rs = -(-side_rows // n_slabs)
        rs = -(-rs // 16) * 16
        last = side_rows // rs - (1 if side_rows % rs == 0 else 0)
        s_map = lambda j, i: (jnp.maximum(j - 1, 0) * nm + i * jnp.minimum(j, 1), 0)
        in_specs.append(pl.BlockSpec((None, rs, side_cols),
                                     lambda j, i: (side_layer, jnp.minimum(s_map(j, i)[0], last), 0)))
        args.append(side)
        out_specs.append(pl.BlockSpec((rs, side_cols), s_map))
        out_shape.append(jax.ShapeDtypeStruct((n_slabs * rs, side_cols), bf16))
    outs = pl.pallas_call(
        functools.partial(_mm_ws_kernel, na=len(a_parts), nw=nw, nt=nt, nm=nm, ck=ck, bn=bn,
                          n_valid=n_cols, act=act,
                          has_res=res is not None, has_side=side is not None, side_rows=side_rows),
        grid=(nt + 1, nm),
        in_specs=in_specs,
        out_specs=out_specs,
        out_shape=out_shape,
        scratch_shapes=[pltpu.VMEM((2 * nw, kdim, bn), bf16)],
        compiler_params=_cparams(("arbitrary", "arbitrary")),
        name="matmul_ws",
    )(*args)
    return outs if side is not None else outs[0]


def _rope(x, c, s_lo, s_hi):
    half = ROPE_DIM // 2
    return x * c + pltpu.roll(x, HEAD_DIM - half, 1) * s_lo + pltpu.roll(x, half, 1) * s_hi


def _qkv_post_kernel(z_ref, c_ref, slo_ref, shi_ref,
                     q_ref, kc_ref, vc_ref, ks_ref, vs_ref, kw_ref, vw_ref):
    c, s_lo, s_hi = c_ref[...], slo_ref[...], shi_ref[...]
    for h in range(NSA_HEADS):
        sl = slice(h * HEAD_DIM, (h + 1) * HEAD_DIM)
        q_ref[:, sl] = (_rope(z_ref[:, sl], c, s_lo, s_hi) * ATTN_SCALE).astype(q_ref.dtype)
    base = NSA_WIDTH
    for idx, (o_ref, roped) in enumerate(((kc_ref, True), (vc_ref, False), (ks_ref, True),
                                          (vs_ref, False), (kw_ref, True), (vw_ref, False))):
        for g in range(NSA_KV_HEADS):
            col = base + idx * KV_WIDTH + g * HEAD_DIM
            x = z_ref[:, col:col + HEAD_DIM]
            if roped:
                x = _rope(x, c, s_lo, s_hi)
            o_ref[:, g * HEAD_DIM:(g + 1) * HEAD_DIM] = x.astype(o_ref.dtype)


def _rope_tables(t_len):
    half = ROPE_DIM // 2
    inv_freq = ROPE_THETA ** (-jnp.arange(half, dtype=f32) * 2.0 / ROPE_DIM)
    ang = jnp.arange(t_len).astype(f32)[:, None] * inv_freq[None, :]
    cos, sin = jnp.cos(ang), jnp.sin(ang)
    ones = jnp.ones((t_len, HEAD_DIM - ROPE_DIM), f32)
    zeros = jnp.zeros((t_len, HEAD_DIM - ROPE_DIM), f32)
    zh = jnp.zeros((t_len, half), f32)
    c = jnp.concatenate([cos, cos, ones], axis=1)
    s_lo = jnp.concatenate([-sin, zh, zeros], axis=1)
    s_hi = jnp.concatenate([zh, sin, zeros], axis=1)
    return c, s_lo, s_hi


def qkv_post(z, t_len, tm=256):
    m = z.shape[0]
    c, s_lo, s_hi = _rope_tables(t_len)
    nt = t_len // tm
    tab = pl.BlockSpec((tm, HEAD_DIM), lambda i: (i % nt, 0))
    kv = pl.BlockSpec((tm, KV_WIDTH), lambda i: (i, 0))
    shp = lambda dt: jax.ShapeDtypeStruct((m, KV_WIDTH), dt)
    return pl.pallas_call(
        _qkv_post_kernel,
        grid=(m // tm,),
        in_specs=[pl.BlockSpec((tm, z.shape[1]), lambda i: (i, 0)), tab, tab, tab],
        out_specs=[pl.BlockSpec((tm, NSA_WIDTH), lambda i: (i, 0)), kv, kv, kv, kv, kv, kv],
        out_shape=[jax.ShapeDtypeStruct((m, NSA_WIDTH), bf16),
                   shp(f32), shp(f32), shp(bf16), shp(bf16), shp(bf16), shp(bf16)],
        compiler_params=_cparams(("parallel",)),
        name="qkv_post",
    )(z, c, s_lo, s_hi)


def _compress_kernel(a_ref, pe_ref, w1_ref, w2_ref, o_ref):
    n_chunks = a_ref.shape[1] // CMP_STRIDE
    top = bot = None
    for l in range(CMP_STRIDE):
        rows = a_ref[0, pl.ds(l, n_chunks, stride=CMP_STRIDE), :]
        t = jnp.dot((rows + pe_ref[l:l + 1, :]).astype(bf16), w1_ref[l].astype(bf16),
                    preferred_element_type=f32)
        u = jnp.dot((rows + pe_ref[CMP_STRIDE + l:CMP_STRIDE + l + 1, :]).astype(bf16),
                    w1_ref[CMP_STRIDE + l].astype(bf16), preferred_element_type=f32)
        top = t if top is None else top + t
        bot = u if bot is None else bot + u
    pre = top + pltpu.roll(bot, n_chunks - 1, 0)
    hid = _gelu_tanh(pre)
    o_ref[0, 0] = jnp.dot(hid.astype(bf16), w2_ref[...].astype(bf16),
                          preferred_element_type=f32).astype(o_ref.dtype)


def compress(a, pe, w1, w2):
    b, t_len, width = a.shape
    n_chunks = t_len // CMP_STRIDE
    return pl.pallas_call(
        _compress_kernel,
        grid=(b, width // HEAD_DIM),
        in_specs=[pl.BlockSpec((1, t_len, HEAD_DIM), lambda i, j: (i, 0, j)),
                  pl.BlockSpec(pe.shape, lambda i, j: (0, 0)),
                  pl.BlockSpec(w1.shape, lambda i, j: (0, 0, 0)),
                  pl.BlockSpec(w2.shape, lambda i, j: (0, 0))],
        out_specs=pl.BlockSpec((1, 1, n_chunks, HEAD_DIM), lambda i, j: (i, j, 0, 0)),
        out_shape=jax.ShapeDtypeStruct((b, width // HEAD_DIM, n_chunks, HEAD_DIM), bf16),
        compiler_params=_cparams(("parallel", "parallel")),
        name="compress",
    )(a, pe, w1, w2)


def _nsa_kernel(q_ref, kc_ref, vc_ref, ks_ref, vs_ref, kw_ref, vw_ref, g_ref, o_ref,
                imp_ref, score_ref, sel_ref, gate_ref):
    tq, ck = ATT_TQ, ATT_CK
    nl = NSA_HPG * tq
    grp = pl.program_id(1)
    qi = pl.program_id(2)
    t0 = qi * tq
    qt = jnp.concatenate([q_ref[0, :, h * HEAD_DIM:(h + 1) * HEAD_DIM].T for h in range(NSA_HPG)],
                         axis=1)
    n_cmp = kc_ref.shape[2]
    n_slc = sel_ref.shape[0]

    def pv(v, pr):
        return lax.dot_general(v, pr.astype(bf16), (((0,), (0,)), ((), ())), preferred_element_type=f32)

    def v_chunk(v_ref, c):
        return v_ref[0, pl.ds(pl.multiple_of(c * ck, ck), ck), :]

    kpos0 = lax.broadcasted_iota(jnp.int32, (ck, tq), 0)
    qpos = t0 + lax.broadcasted_iota(jnp.int32, (ck, tq), 1)
    c_last = (t0 + tq - 1) // ck

    def raw_scores(k_ref, c):
        k = k_ref[0, pl.ds(pl.multiple_of(c * ck, ck), ck), :]
        return jnp.dot(k, qt, preferred_element_type=f32)

    def masked(sc, mask):
        return jnp.where(jnp.concatenate([mask] * NSA_HPG, axis=1), sc, NEG)

    s = jnp.dot(kc_ref[0, 0], qt, preferred_element_type=f32)
    n_win = 1 + -(-(WINDOW - 1) // ck)
    c_win = [c_last - back for back in range(n_win)]
    s_win = [raw_scores(kw_ref, jnp.maximum(c, 0)) for c in c_win]

    def pair_scores(c):
        return raw_scores(ks_ref, c), raw_scores(ks_ref, c + 1)

    blk_end = lax.broadcasted_iota(jnp.int32, (n_cmp, nl), 0) * CMP_STRIDE + (CMP_BLOCK - 1)
    qpos4 = t0 + lax.broadcasted_iota(jnp.int32, (n_cmp, nl), 1) % tq
    valid = blk_end <= qpos4
    s = jnp.where(valid, s, NEG)
    mx = jnp.max(s, axis=0, keepdims=True)
    e = jnp.where(valid, jnp.exp(s - mx), 0.0)
    den = jnp.sum(e, axis=0, keepdims=True)
    p = e / jnp.where(den > 0.0, den, 1.0)
    o_cmp = pv(vc_ref[0, 0], p)
    imp = (p[:, 0:tq] + p[:, tq:2 * tq]) + p[:, 2 * tq:3 * tq] + p[:, 3 * tq:4 * tq]
    for lt in range(tq // LANE):
        imp_ref[lt] = imp[:, lt * LANE:(lt + 1) * LANE]

    for i, c in enumerate(c_win):
        kpos = kpos0 + c * ck
        s_win[i] = masked(s_win[i], (kpos <= qpos) & (kpos > qpos - WINDOW) & (kpos >= 0))
    m_w = functools.reduce(jnp.maximum, [jnp.max(sw, axis=0, keepdims=True) for sw in s_win])
    p_win = [jnp.exp(sw - m_w) for sw in s_win]
    l_w = functools.reduce(jnp.add, [jnp.sum(pw, axis=0, keepdims=True) for pw in p_win])
    acc_w = functools.reduce(jnp.add, [pv(v_chunk(vw_ref, jnp.maximum(c, 0)), pw)
                                       for c, pw in zip(c_win, p_win)])
    o_win = acc_w / l_w

    r = SLC_BLOCK // CMP_STRIDE
    rows = [jnp.concatenate([imp_ref[lt, pl.ds(k, n_slc, stride=r), :] for lt in range(tq // LANE)],
                            axis=1) for k in range(r)]
    blk = lax.broadcasted_iota(jnp.int32, (n_slc, tq), 0)
    prev = jnp.where(blk == 0, 0.0, pltpu.roll(rows[r - 1], 1, 0))
    score = (rows[0] + rows[1] + rows[2]) + 0.5 * rows[r - 1] + 0.5 * prev
    cur = (t0 + lax.broadcasted_iota(jnp.int32, (n_slc, tq), 1)) // SLC_BLOCK
    score = jnp.where((blk == cur) | (blk == 0), -NEG, jnp.where(blk > cur, NEG, score))
    score_ref[...] = score
    cnt = jnp.zeros((n_slc, tq), jnp.int32)
    for jp in range(n_slc):
        row = jnp.broadcast_to(score_ref[pl.ds(jp, 1), :], (n_slc, tq))
        beats = (row > score) | ((row == score) & (blk > jp))
        cnt = cnt + beats.astype(jnp.int32)
    sel_ref[...] = jnp.where(cnt < min(SLC_TOPK, n_slc), 1.0, 0.0)

    init = (jnp.full((1, nl), NEG, f32), jnp.zeros((1, nl), f32), jnp.zeros((HEAD_DIM, nl), f32))

    def flash_update(c, sc, carry):
        m, l, acc = carry
        per_chunk = ck // SLC_BLOCK
        selrows = jnp.concatenate(
            [jnp.broadcast_to(sel_ref[pl.ds(c * per_chunk + i, 1), :], (SLC_BLOCK, tq))
             for i in range(per_chunk)], axis=0)
        sc = masked(sc, (selrows > 0.5) & (kpos0 + c * ck <= qpos))
        m_new = jnp.maximum(m, jnp.max(sc, axis=0, keepdims=True))
        alpha = jnp.exp(m - m_new)
        pr = jnp.exp(sc - m_new)
        l = alpha * l + jnp.sum(pr, axis=0, keepdims=True)
        acc = alpha * acc + pv(v_chunk(vs_ref, c), pr)
        return m_new, l, acc

    def pair_body(i, carry):
        s_a, s_b = pair_scores(2 * i)
        return flash_update(2 * i, s_a, carry[0]), flash_update(2 * i + 1, s_b, carry[1])

    (m_a, l_a, acc_a), (m_b, l_b, acc_b) = lax.fori_loop(0, (c_last + 2) // 2, pair_body, (init, init))
    m_s = jnp.maximum(m_a, m_b)
    w_a, w_b = jnp.exp(m_a - m_s), jnp.exp(m_b - m_s)
    o_slc = (w_a * acc_a + w_b * acc_b) / (w_a * l_a + w_b * l_b)

    gate_ref[...] = jax.nn.sigmoid(g_ref[...].T)
    for h in range(NSA_HPG):
        sl = slice(h * tq, (h + 1) * tq)
        g_c, g_s, g_w = [gate_ref[pl.ds((grp * NSA_HPG + h) * 3 + br, 1), :] for br in range(3)]
        o_h = g_c * o_cmp[:, sl] + g_s * o_slc[:, sl] + g_w * o_win[:, sl]
        o_ref[0, :, h * HEAD_DIM:(h + 1) * HEAD_DIM] = o_h.T.astype(o_ref.dtype)


def nsa_attention(q, k_cmp, v_cmp, ks, vs, kw, vw, gates, t_len):
    b = q.shape[0] // t_len
    tq, ck = ATT_TQ, ATT_CK
    nq = t_len // tq
    n_cmp = k_cmp.shape[2]
    n_slc = t_len // SLC_BLOCK
    assert (t_len // ck) % 2 == 0
    seq = lambda a: a.reshape(b, t_len, a.shape[1])
    kv_full = pl.BlockSpec((1, t_len, HEAD_DIM), lambda i, g, j: (i, 0, g))
    cmp_full = pl.BlockSpec((1, 1, n_cmp, HEAD_DIM), lambda i, g, j: (i, g, 0, 0))
    return pl.pallas_call(
        _nsa_kernel,
        grid=(b, NSA_KV_HEADS, nq),
        in_specs=[pl.BlockSpec((1, tq, NSA_HPG * HEAD_DIM), lambda i, g, j: (i, j, g)),
                  cmp_full, cmp_full, kv_full, kv_full, kv_full, kv_full,
                  pl.BlockSpec((tq, LANE), lambda i, g, j: (i * nq + j, 0))],
        out_specs=pl.BlockSpec((1, tq, NSA_HPG * HEAD_DIM), lambda i, g, j: (i, j, g)),
        out_shape=jax.ShapeDtypeStruct((b, t_len, NSA_WIDTH), bf16),
        scratch_shapes=[pltpu.VMEM((tq // LANE, n_cmp, LANE), f32), pltpu.VMEM((n_slc, tq), f32),
                        pltpu.VMEM((n_slc, tq), f32), pltpu.VMEM((LANE, tq), f32)],
        compiler_params=_cparams(("parallel", "parallel", "arbitrary")),
        name="nsa_attention",
    )(seq(q), k_cmp, v_cmp, seq(ks), seq(vs), seq(kw), seq(vw), gates)


def _pool_kernel(x_ref, halo_ref, w_ref, scale_ref, o_ref, *, t_len):
    tm = x_ref.shape[0]
    i = pl.program_id(0)
    t_start = (i * tm) % t_len
    halo = jnp.where(t_start == 0, 0.0, halo_ref[...])
    tpos = t_start + lax.broadcasted_iota(jnp.int32, (tm, 1), 0)
    kw = w_ref.shape[1]
    for gi, w_len in enumerate(POOL_WINDOWS):
        sl = slice(gi * POOL_GROUP, (gi + 1) * POOL_GROUP)
        xs = slice(gi * POOL_GROUP, gi * POOL_GROUP + kw)
        x = x_ref[:, xs]
        acc = jnp.concatenate([halo[:, xs], x], axis=0)
        first = -POOL_HALO
        span = 1
        while span < w_len:
            acc = acc[span:, :] + acc[:-span, :]
            first += span
            span *= 2
        wsum = acc[-first:-first + tm, :]
        count = jnp.minimum(tpos + 1, w_len).astype(f32)
        y = (wsum / count - x).astype(bf16)
        y = jnp.dot(y, w_ref[gi].astype(bf16), preferred_element_type=f32)
        o_ref[:, sl] = (y * scale_ref[:, sl]).astype(o_ref.dtype)


def pool_mixer(x, w, scale, t_len, lane0=0, tm=256):
    m = x.shape[0]
    groups, cg, _ = w.shape
    c = groups * cg
    kw = -(-(lane0 + cg) // LANE) * LANE
    xc = (groups - 1) * cg + kw
    assert lane0 < LANE and xc <= x.shape[1]
    w_shift = jnp.pad(w, ((0, 0), (lane0, kw - lane0 - cg), (0, 0)))
    hb = tm // POOL_HALO
    return pl.pallas_call(
        functools.partial(_pool_kernel, t_len=t_len),
        grid=(m // tm,),
        in_specs=[pl.BlockSpec((tm, xc), lambda i: (i, 0)),
                  pl.BlockSpec((POOL_HALO, xc), lambda i: (jnp.maximum(i * hb - 1, 0), 0)),
                  pl.BlockSpec(w_shift.shape, lambda i: (0, 0, 0)),
                  pl.BlockSpec((1, c), lambda i: (0, 0))],
        out_specs=pl.BlockSpec((tm, c), lambda i: (i, 0)),
        out_shape=jax.ShapeDtypeStruct((m, c), bf16),
        compiler_params=_cparams(("parallel",)),
        name="pool_mixer",
    )(x, x, w_shift, scale.reshape(1, c))


def _gmlp_kernel(u_ref, v_ref, lg_ref, lb_ref, ws_ref, bs_ref, o_ref, vn_ref):
    tc = v_ref.shape[0]
    vf = v_ref[...].astype(f32)
    mu = jnp.mean(vf, axis=-1, keepdims=True)
    d = vf - mu
    var = jnp.mean(d * d, axis=-1, keepdims=True)
    vn_ref[...] = (d * lax.rsqrt(var + EPS) * lg_ref[...] + lb_ref[...]).astype(bf16)
    gd = v_ref.shape[1] // GMLP_GROUPS
    tri = lax.broadcasted_iota(jnp.int32, (tc, tc), 0) >= lax.broadcasted_iota(jnp.int32, (tc, tc), 1)
    for g in range(GMLP_GROUPS):
        sl = slice(g * gd, (g + 1) * gd)
        w = jnp.where(tri, ws_ref[g], 0.0).astype(bf16)
        s = jnp.dot(w, vn_ref[:, sl], preferred_element_type=f32) + bs_ref[:, g:g + 1]
        o_ref[:, sl] = (u_ref[:, sl].astype(f32) * s).astype(o_ref.dtype)


def gmlp_gate(z, ln_g, ln_b, ws, bs):
    m, two_w = z.shape
    width = two_w // 2
    tc = GMLP_CHUNK
    row = pl.BlockSpec((1, width), lambda i: (0, 0))
    return pl.pallas_call(
        _gmlp_kernel,
        grid=(m // tc,),
        in_specs=[pl.BlockSpec((tc, width), lambda i: (i, 0)),
                  pl.BlockSpec((tc, width), lambda i: (i, 1)),
                  row, row,
                  pl.BlockSpec(ws.shape, lambda i: (0, 0, 0)),
                  pl.BlockSpec((tc, GMLP_GROUPS), lambda i: (0, 0))],
        out_specs=pl.BlockSpec((tc, width), lambda i: (i, 0)),
        out_shape=jax.ShapeDtypeStruct((m, width), bf16),
        scratch_shapes=[pltpu.VMEM((tc, width), bf16)],
        compiler_params=_cparams(("parallel",)),
        name="gmlp_gate",
    )(z, z, ln_g.reshape(1, width), ln_b.reshape(1, width), ws, bs.T)


def _ffn(x, norm_g, wg, wu, wd, layer):
    h = rms_norm(x, norm_g, bf16)
    gu, wd_bf = matmul_ws(h, (wg, wu), layer, wg.shape[2], out_dtype=bf16, bn=FFN_PAD,
                          side=wd, side_layer=layer)
    return matmul(gu, wd_bf, res=x)


def _even_layer(x, b, t_len, norm_g, w_in, j, pe_k, pe_v, w1_k, w2_k, w1_v, w2_v, pool_w, pool_scale, w_out):
    m = x.shape[0]
    h = rms_norm(x, norm_g, bf16)
    qkv_w = NSA_WIDTH + 6 * KV_WIDTH
    z = matmul_ws(h, (w_in,), j, qkv_w)
    zgp = matmul_ws(h, (w_in,), j, w_in.shape[2] - qkv_w, col0=qkv_w, bn=512)

    q, kc, vc, ks, vs, kw, vw = qkv_post(z, t_len)
    seq = lambda a: a.reshape(b, t_len, a.shape[1])
    k_cmp = compress(seq(kc), pe_k, w1_k, w2_k)
    v_cmp = compress(seq(vc), pe_v, w1_v, w2_v)
    o = nsa_attention(q, k_cmp, v_cmp, ks, vs, kw, vw, zgp, t_len)
    y_pool = pool_mixer(zgp, pool_w, pool_scale, t_len, lane0=N_GATES)
    return matmul_ws((o.reshape(m, NSA_WIDTH), y_pool), (w_out,), j, w_out.shape[2], res=x, bn=512)


def _odd_layer(x, norm_g, w_in, j, ln_g, ln_b, ws, bs, w_out):
    h = rms_norm(x, norm_g, bf16)
    z, w_out_bf = matmul_ws(h, (w_in,), j, w_in.shape[2], act="gelu", out_dtype=bf16,
                            side=w_out, side_layer=j)
    y = gmlp_gate(z, ln_g, ln_b, ws, bs)
    return matmul(y, w_out_bf, res=x)


def kernel(x, norm_mix_even, w_in_even, cmp_pe_k, cmp_pe_v, cmp_w1_k, cmp_w2_k, cmp_w1_v, cmp_w2_v, pool_w, pool_scale, w_out_even, norm_mix_odd, w_in_odd, gmlp_ln_g, gmlp_ln_b, gmlp_ws, gmlp_bs, w_out_odd, norm_ffn, w_ffn_gate, w_ffn_up, w_ffn_down, norm_final):
    b, t_len, d = x.shape
    depth = norm_ffn.shape[0]
    xf = x.reshape(b * t_len, d)
    for layer in range(depth):
        j = layer // 2
        if layer % 2 == 0:
            xf = _even_layer(xf, b, t_len, norm_mix_even[j], w_in_even, j, cmp_pe_k[j], cmp_pe_v[j],
                             cmp_w1_k[j], cmp_w2_k[j], cmp_w1_v[j], cmp_w2_v[j], pool_w[j],
                             pool_scale[j], w_out_even)
        else:
            xf = _odd_layer(xf, norm_mix_odd[j], w_in_odd, j, gmlp_ln_g[j], gmlp_ln_b[j], gmlp_ws[j],
                            gmlp_bs[j], w_out_odd)
        xf = _ffn(xf, norm_ffn[layer], w_ffn_gate, w_ffn_up, w_ffn_down, layer)
    return rms_norm(xf, norm_final, x.dtype).reshape(b, t_len, d)
```

```python
import functools

import jax
import jax.numpy as jnp
from jax import lax
from jax.experimental import pallas as pl
from jax.experimental.pallas import tpu as pltpu

HEAD_DIM = 128
NSA_HEADS = 16
NSA_KV_HEADS = 4
NSA_HPG = NSA_HEADS // NSA_KV_HEADS
NSA_WIDTH = NSA_HEADS * HEAD_DIM
KV_WIDTH = NSA_KV_HEADS * HEAD_DIM
CMP_STRIDE = 16
CMP_BLOCK = 32
CMP_HIDDEN = 256
SLC_BLOCK = 64
SLC_TOPK = 16
WINDOW = 512
N_GATES = 3 * NSA_HEADS
ATTN_SCALE = HEAD_DIM ** -0.5
ROPE_THETA = 500000.0
ROPE_DIM = HEAD_DIM // 4
POOL_WINDOWS = (2, 4, 8, 16)
POOL_GROUP = 512
POOL_HALO = 16
GMLP_CHUNK = 128
GMLP_GROUPS = 16
EPS = 1e-6
NEG = -1e30

LANE = 128
MXU_COLS = 256
VMEM_LIMIT_BYTES = 56 * 1024 * 1024
ATT_TQ = 256
ATT_CK = 256
FFN_PAD = 512
MM_VMEM_BUDGET_BYTES = 48 * 1024 * 1024

bf16 = jnp.bfloat16
f32 = jnp.float32


def _cparams(sem):
    return pltpu.CompilerParams(dimension_semantics=sem, vmem_limit_bytes=VMEM_LIMIT_BYTES)


def _rms_kernel(x_ref, g_ref, o_ref):
    x = x_ref[...]
    ms = jnp.mean(x * x, axis=-1, keepdims=True)
    o_ref[...] = (x * lax.rsqrt(ms + EPS) * g_ref[...]).astype(o_ref.dtype)


def rms_norm(x, g, out_dtype, tm=256):
    m, d = x.shape
    return pl.pallas_call(
        _rms_kernel,
        grid=(m // tm,),
        in_specs=[pl.BlockSpec((tm, d), lambda i: (i, 0)), pl.BlockSpec((1, d), lambda i: (0, 0))],
        out_specs=pl.BlockSpec((tm, d), lambda i: (i, 0)),
        out_shape=jax.ShapeDtypeStruct((m, d), out_dtype),
        compiler_params=_cparams(("parallel",)),
        name="rms_norm",
    )(x, g.reshape(1, d))


def _gelu_tanh(x):
    c0 = 2.0 * 0.7978845608028654
    return x * jax.nn.sigmoid(x * (c0 + (c0 * 0.044715) * (x * x)))


def _mm_kernel(*refs, nk, has_res, act):
    if has_res:
        a_ref, w_ref, r_ref, o_ref = refs[:4]
        scratch = refs[4:]
    else:
        a_ref, w_ref, o_ref = refs[:3]
        r_ref = None
        scratch = refs[3:]

    bn = o_ref.shape[1]
    sub = min(bn, MXU_COLS)

    def step(first, last):
        for s in range(bn // sub):
            cs = slice(s * sub, (s + 1) * sub)
            d = jnp.dot(a_ref[...], w_ref[:, cs], preferred_element_type=f32)
            if not first:
                d = scratch[0][:, cs] + d
            if not last:
                scratch[0][:, cs] = d
                continue
            if act == "gelu":
                d = _gelu_tanh(d)
            if has_res:
                d = d + r_ref[:, cs]
            o_ref[:, cs] = d.astype(o_ref.dtype)

    if nk == 1:
        step(True, True)
        return
    k = pl.program_id(2)
    pl.when(k == 0)(functools.partial(step, True, False))
    pl.when(jnp.logical_and(k > 0, k < nk - 1))(functools.partial(step, False, False))
    pl.when(k == nk - 1)(functools.partial(step, False, True))


def _pick_bk(kdim, bm, bn, out_bytes, has_res):
    fixed = bm * bn * (4 + 2 * out_bytes + (8 if has_res else 0))
    for nk in range(1, kdim // LANE + 1):
        bk = kdim // nk
        acc = bm * bn * 4 if nk > 1 else 0
        if kdim % nk == 0 and bk % LANE == 0 and fixed + acc + 4 * bk * (bm + bn) <= MM_VMEM_BUDGET_BYTES:
            return bk
    raise ValueError("no K tile fits")


def matmul(a, w, res=None, act=None, out_dtype=f32, bm=1024, bn=1024):
    m, kdim = a.shape
    _, n = w.shape
    bn = min(bn, n)
    bk = _pick_bk(kdim, bm, bn, jnp.dtype(out_dtype).itemsize, res is not None)
    assert m % bm == 0 and n % bn == 0 and kdim % bk == 0
    nk = kdim // bk
    in_specs = [pl.BlockSpec((bm, bk), lambda j, i, k: (i, k)),
                pl.BlockSpec((bk, bn), lambda j, i, k: (k, j))]
    args = [a, w]
    if res is not None:
        in_specs.append(pl.BlockSpec((bm, bn), lambda j, i, k: (i, j)))
        args.append(res)
    scratch = [pltpu.VMEM((bm, bn), f32)] if nk > 1 else []
    return pl.pallas_call(
        functools.partial(_mm_kernel, nk=nk, has_res=res is not None, act=act),
        grid=(n // bn, m // bm, nk),
        in_specs=in_specs,
        out_specs=pl.BlockSpec((bm, bn), lambda j, i, k: (i, j)),
        out_shape=jax.ShapeDtypeStruct((m, n), out_dtype),
        scratch_shapes=scratch,
        compiler_params=_cparams(("parallel", "parallel", "arbitrary")),
        name="matmul",
    )(*args)


def _mm_ws_kernel(*refs, na, nw, nt, nm, ck, bn, n_valid, act, has_res, has_side, side_rows,
                  w_transposed):
    refs = list(refs)
    a_refs = [refs.pop(0) for _ in range(na)]
    w_refs = [refs.pop(0) for _ in range(nw)]
    r_ref = refs.pop(0) if has_res else None
    s_ref = refs.pop(0) if has_side else None
    o_ref = refs.pop(0)
    so_ref = refs.pop(0) if has_side else None
    wbf_ref = refs.pop(0)
    j = pl.program_id(0)
    i = pl.program_id(1)

    def cast(edge):
        for t in range(nw):
            w = w_refs[t][...]
            if edge:
                col = j * bn + lax.broadcasted_iota(jnp.int32, w.shape, 0 if w_transposed else 1)
                w = jnp.where(col < n_valid, w, 0.0)
            if w_transposed:
                w = w.T
            wbf_ref[(j % 2) * nw + t, pl.ds(pl.multiple_of(i * ck, ck), ck), :] = w.astype(bf16)

    if n_valid % bn:
        pl.when(j < nt - 1)(functools.partial(cast, False))
        pl.when(j == nt - 1)(functools.partial(cast, True))
    else:
        pl.when(j < nt)(functools.partial(cast, False))

    @pl.when(j == 0)
    def _():
        o_ref[...] = jnp.zeros(o_ref.shape, o_ref.dtype)

    @pl.when(j > 0)
    def _():
        slot = ((j - 1) % 2) * nw
        sub = min(bn, MXU_COLS)

        def lhs_dot(w_slot, cs):
            k0, acc = 0, None
            for a_ref in a_refs:
                kp = a_ref.shape[1]
                part = jnp.dot(a_ref[...], wbf_ref[w_slot, k0:k0 + kp, cs], preferred_element_type=f32)
                acc = part if acc is None else acc + part
                k0 += kp
            return acc

        for s in range(bn // sub):
            cs = slice(s * sub, (s + 1) * sub)
            d = lhs_dot(slot, cs)
            if nw == 2:
                d = d * jax.nn.sigmoid(d) * lhs_dot(slot + 1, cs)
            if act == "gelu":
                d = _gelu_tanh(d)
            if has_res:
                d = d + r_ref[:, cs]
            o_ref[:, cs] = d.astype(o_ref.dtype)

    if has_side:
        rs = s_ref.shape[0]
        slab = jnp.maximum(j - 1, 0) * nm + i * jnp.minimum(j, 1)
        full = slab < side_rows // rs

        @pl.when(full)
        def _():
            so_ref[...] = s_ref[...].astype(so_ref.dtype)

        @pl.when(jnp.logical_not(full))
        def _():
            row = slab * rs + lax.broadcasted_iota(jnp.int32, s_ref.shape, 0)
            so_ref[...] = jnp.where(row < side_rows, s_ref[...], 0.0).astype(so_ref.dtype)


def matmul_ws(a, ws, layer, n_cols, col0=0, res=None, act=None, out_dtype=f32, bm=1024, bn=1024,
              side=None, side_layer=0, w_transposed=False):
    a_parts = a if isinstance(a, (tuple, list)) else (a,)
    m = a_parts[0].shape[0]
    kdim = sum(p.shape[1] for p in a_parts)
    nw = len(ws)
    nm = m // bm
    ck = kdim // nm
    nt = pl.cdiv(n_cols, bn)
    ct0 = col0 // bn
    assert m % bm == 0 and kdim % nm == 0 and ck % 8 == 0 and col0 % bn == 0
    a_specs = [pl.BlockSpec((bm, p.shape[1]), lambda j, i: (i * jnp.minimum(j, 1), 0)) for p in a_parts]
    if w_transposed:
        w_spec = pl.BlockSpec((None, bn, ck), lambda j, i: (layer, ct0 + jnp.minimum(j, nt - 1), i))
    else:
        w_spec = pl.BlockSpec((None, ck, bn), lambda j, i: (layer, i, ct0 + jnp.minimum(j, nt - 1)))
    o_map = lambda j, i: (i * jnp.minimum(j, 1), jnp.maximum(j - 1, 0))
    in_specs = a_specs + [w_spec] * nw
    args = list(a_parts) + list(ws)
    if res is not None:
        in_specs.append(pl.BlockSpec((bm, bn), o_map))
        args.append(res)
    out_specs = [pl.BlockSpec((bm, bn), o_map)]
    out_shape = [jax.ShapeDtypeStruct((m, nt * bn), out_dtype)]
    side_rows = 0
    if side is not None:
        _, side_rows, side_cols = side.shape
        n_slabs = nt * nm
        rs = -(-side_rows // n_slabs)
        rs = -(-rs // 16) * 16
        last = side_rows // rs - (1 if side_rows % rs == 0 else 0)
        s_map = lambda j, i: (jnp.maximum(j - 1, 0) * nm + i * jnp.minimum(j, 1), 0)
        in_specs.append(pl.BlockSpec((None, rs, side_cols),
                                     lambda j, i: (side_layer, jnp.minimum(s_map(j, i)[0], last), 0)))
        args.append(side)
        out_specs.append(pl.BlockSpec((rs, side_cols), s_map))
        out_shape.append(jax.ShapeDtypeStruct((n_slabs * rs, side_cols), bf16))
    outs = pl.pallas_call(
        functools.partial(_mm_ws_kernel, na=len(a_parts), nw=nw, nt=nt, nm=nm, ck=ck, bn=bn,
                          n_valid=n_cols, act=act,
                          has_res=res is not None, has_side=side is not None, side_rows=side_rows,
                          w_transposed=w_transposed),
        grid=(nt + 1, nm),
        in_specs=in_specs,
        out_specs=out_specs,
        out_shape=out_shape,
        scratch_shapes=[pltpu.VMEM((2 * nw, kdim, bn), bf16)],
        compiler_params=_cparams(("arbitrary", "arbitrary")),
        name="matmul_ws",
    )(*args)
    return outs if side is not None else outs[0]


def _rope(x, c, s_lo, s_hi):
    half = ROPE_DIM // 2
    return x * c + pltpu.roll(x, HEAD_DIM - half, 1) * s_lo + pltpu.roll(x, half, 1) * s_hi


def _qkv_post_kernel(z_ref, c_ref, slo_ref, shi_ref, q_ref, kc_ref, vc_ref, ks_ref, kw_ref):
    c, s_lo, s_hi = c_ref[...], slo_ref[...], shi_ref[...]
    for h in range(NSA_HEADS):
        sl = slice(h * HEAD_DIM, (h + 1) * HEAD_DIM)
        q_ref[:, sl] = (_rope(z_ref[:, sl].astype(f32), c, s_lo, s_hi) * ATTN_SCALE).astype(q_ref.dtype)
    for idx, o_ref, roped in ((0, kc_ref, True), (1, vc_ref, False), (2, ks_ref, True), (4, kw_ref, True)):
        for g in range(NSA_KV_HEADS):
            col = NSA_WIDTH + idx * KV_WIDTH + g * HEAD_DIM
            x = z_ref[:, col:col + HEAD_DIM].astype(f32)
            if roped:
                x = _rope(x, c, s_lo, s_hi)
            o_ref[:, g * HEAD_DIM:(g + 1) * HEAD_DIM] = x.astype(o_ref.dtype)


def _rope_tables(t_len):
    half = ROPE_DIM // 2
    inv_freq = ROPE_THETA ** (-jnp.arange(half, dtype=f32) * 2.0 / ROPE_DIM)
    ang = jnp.arange(t_len).astype(f32)[:, None] * inv_freq[None, :]
    cos, sin = jnp.cos(ang), jnp.sin(ang)
    ones = jnp.ones((t_len, HEAD_DIM - ROPE_DIM), f32)
    zeros = jnp.zeros((t_len, HEAD_DIM - ROPE_DIM), f32)
    zh = jnp.zeros((t_len, half), f32)
    c = jnp.concatenate([cos, cos, ones], axis=1)
    s_lo = jnp.concatenate([-sin, zh, zeros], axis=1)
    s_hi = jnp.concatenate([zh, sin, zeros], axis=1)
    return c, s_lo, s_hi


def qkv_post(z, t_len, tm=256):
    m = z.shape[0]
    c, s_lo, s_hi = _rope_tables(t_len)
    nt = t_len // tm
    tab = pl.BlockSpec((tm, HEAD_DIM), lambda i: (i % nt, 0))
    kv = pl.BlockSpec((tm, KV_WIDTH), lambda i: (i, 0))
    shp = lambda dt: jax.ShapeDtypeStruct((m, KV_WIDTH), dt)
    return pl.pallas_call(
        _qkv_post_kernel,
        grid=(m // tm,),
        in_specs=[pl.BlockSpec((tm, z.shape[1]), lambda i: (i, 0)), tab, tab, tab],
        out_specs=[pl.BlockSpec((tm, NSA_WIDTH), lambda i: (i, 0)), kv, kv, kv, kv],
        out_shape=[jax.ShapeDtypeStruct((m, NSA_WIDTH), bf16), shp(f32), shp(f32), shp(bf16), shp(bf16)],
        compiler_params=_cparams(("parallel",)),
        name="qkv_post",
    )(z, c, s_lo, s_hi)


def _compress_kernel(a_ref, pe_ref, w1_ref, w2_ref, o_ref):
    n_chunks = a_ref.shape[1] // CMP_STRIDE
    top = bot = None
    for l in range(CMP_STRIDE):
        rows = a_ref[0, pl.ds(l, n_chunks, stride=CMP_STRIDE), :]
        t = jnp.dot((rows + pe_ref[l:l + 1, :]).astype(bf16), w1_ref[l].astype(bf16),
                    preferred_element_type=f32)
        u = jnp.dot((rows + pe_ref[CMP_STRIDE + l:CMP_STRIDE + l + 1, :]).astype(bf16),
                    w1_ref[CMP_STRIDE + l].astype(bf16), preferred_element_type=f32)
        top = t if top is None else top + t
        bot = u if bot is None else bot + u
    pre = top + pltpu.roll(bot, n_chunks - 1, 0)
    hid = _gelu_tanh(pre)
    o_ref[0, 0] = jnp.dot(hid.astype(bf16), w2_ref[...].astype(bf16),
                          preferred_element_type=f32).astype(o_ref.dtype)


def compress(a, pe, w1, w2):
    b, t_len, width = a.shape
    n_chunks = t_len // CMP_STRIDE
    return pl.pallas_call(
        _compress_kernel,
        grid=(b, width // HEAD_DIM),
        in_specs=[pl.BlockSpec((1, t_len, HEAD_DIM), lambda i, j: (i, 0, j)),
                  pl.BlockSpec(pe.shape, lambda i, j: (0, 0)),
                  pl.BlockSpec(w1.shape, lambda i, j: (0, 0, 0)),
                  pl.BlockSpec(w2.shape, lambda i, j: (0, 0))],
        out_specs=pl.BlockSpec((1, 1, n_chunks, HEAD_DIM), lambda i, j: (i, j, 0, 0)),
        out_shape=jax.ShapeDtypeStruct((b, width // HEAD_DIM, n_chunks, HEAD_DIM), bf16),
        compiler_params=_cparams(("parallel", "parallel")),
        name="compress",
    )(a, pe, w1, w2)


def _nsa_kernel(q_ref, kc_ref, vc_ref, ks_ref, vs_ref, kw_ref, vw_ref, g_ref, o_ref,
                imp_ref, score_ref, sel_ref, gate_ref):
    tq, ck = ATT_TQ, ATT_CK
    nl = NSA_HPG * tq
    grp = pl.program_id(1)
    qi = pl.program_id(2)
    t0 = qi * tq
    qt = jnp.concatenate([q_ref[0, :, h * HEAD_DIM:(h + 1) * HEAD_DIM].T for h in range(NSA_HPG)],
                         axis=1)
    n_cmp = kc_ref.shape[2]
    n_slc = sel_ref.shape[0]

    def pv(v, pr):
        return lax.dot_general(v, pr.astype(bf16), (((0,), (0,)), ((), ())), preferred_element_type=f32)

    def v_chunk(v_ref, c):
        return v_ref[0, pl.ds(pl.multiple_of(c * ck, ck), ck), :]

    kpos0 = lax.broadcasted_iota(jnp.int32, (ck, tq), 0)
    qpos = t0 + lax.broadcasted_iota(jnp.int32, (ck, tq), 1)
    c_last = (t0 + tq - 1) // ck

    def raw_scores(k_ref, c):
        k = k_ref[0, pl.ds(pl.multiple_of(c * ck, ck), ck), :]
        return jnp.dot(k, qt, preferred_element_type=f32)

    def masked(sc, mask):
        return jnp.where(jnp.concatenate([mask] * NSA_HPG, axis=1), sc, NEG)

    s = jnp.dot(kc_ref[0, 0], qt, preferred_element_type=f32)
    n_win = 1 + -(-(WINDOW - 1) // ck)
    c_win = [c_last - back for back in range(n_win)]
    s_win = [raw_scores(kw_ref, jnp.maximum(c, 0)) for c in c_win]

    def pair_scores(c):
        return raw_scores(ks_ref, c), raw_scores(ks_ref, c + 1)

    blk_end = lax.broadcasted_iota(jnp.int32, (n_cmp, nl), 0) * CMP_STRIDE + (CMP_BLOCK - 1)
    qpos4 = t0 + lax.broadcasted_iota(jnp.int32, (n_cmp, nl), 1) % tq
    valid = blk_end <= qpos4
    s = jnp.where(valid, s, NEG)
    mx = jnp.max(s, axis=0, keepdims=True)
    e = jnp.where(valid, jnp.exp(s - mx), 0.0)
    den = jnp.sum(e, axis=0, keepdims=True)
    p = e / jnp.where(den > 0.0, den, 1.0)
    o_cmp = pv(vc_ref[0, 0], p)
    imp = (p[:, 0:tq] + p[:, tq:2 * tq]) + p[:, 2 * tq:3 * tq] + p[:, 3 * tq:4 * tq]
    for lt in range(tq // LANE):
        imp_ref[lt] = imp[:, lt * LANE:(lt + 1) * LANE]

    for i, c in enumerate(c_win):
        kpos = kpos0 + c * ck
        s_win[i] = masked(s_win[i], (kpos <= qpos) & (kpos > qpos - WINDOW) & (kpos >= 0))
    m_w = functools.reduce(jnp.maximum, [jnp.max(sw, axis=0, keepdims=True) for sw in s_win])
    p_win = [jnp.exp(sw - m_w) for sw in s_win]
    l_w = functools.reduce(jnp.add, [jnp.sum(pw, axis=0, keepdims=True) for pw in p_win])
    acc_w = functools.reduce(jnp.add, [pv(v_chunk(vw_ref, jnp.maximum(c, 0)), pw)
                                       for c, pw in zip(c_win, p_win)])
    o_win = acc_w / l_w

    r = SLC_BLOCK // CMP_STRIDE
    rows = [jnp.concatenate([imp_ref[lt, pl.ds(k, n_slc, stride=r), :] for lt in range(tq // LANE)],
                            axis=1) for k in range(r)]
    blk = lax.broadcasted_iota(jnp.int32, (n_slc, tq), 0)
    prev = jnp.where(blk == 0, 0.0, pltpu.roll(rows[r - 1], 1, 0))
    score = (rows[0] + rows[1] + rows[2]) + 0.5 * rows[r - 1] + 0.5 * prev
    cur = (t0 + lax.broadcasted_iota(jnp.int32, (n_slc, tq), 1)) // SLC_BLOCK
    score = jnp.where((blk == cur) | (blk == 0), -NEG, jnp.where(blk > cur, NEG, score))
    score_ref[...] = score
    cnt = jnp.zeros((n_slc, tq), jnp.int32)
    for jp in range(n_slc):
        row = jnp.broadcast_to(score_ref[pl.ds(jp, 1), :], (n_slc, tq))
        beats = (row > score) | ((row == score) & (blk > jp))
        cnt = cnt + beats.astype(jnp.int32)
    sel_ref[...] = jnp.where(cnt < min(SLC_TOPK, n_slc), 1.0, 0.0)

    init = (jnp.full((1, nl), NEG, f32), jnp.zeros((1, nl), f32), jnp.zeros((HEAD_DIM, nl), f32))

    def flash_update(c, sc, carry):
        m, l, acc = carry
        per_chunk = ck // SLC_BLOCK
        selrows = jnp.concatenate(
            [jnp.broadcast_to(sel_ref[pl.ds(c * per_chunk + i, 1), :], (SLC_BLOCK, tq))
             for i in range(per_chunk)], axis=0)
        sc = masked(sc, (selrows > 0.5) & (kpos0 + c * ck <= qpos))
        m_new = jnp.maximum(m, jnp.max(sc, axis=0, keepdims=True))
        alpha = jnp.exp(m - m_new)
        pr = jnp.exp(sc - m_new)
        l = alpha * l + jnp.sum(pr, axis=0, keepdims=True)
        acc = alpha * acc + pv(v_chunk(vs_ref, c), pr)
        return m_new, l, acc

    def pair_body(i, carry):
        s_a, s_b = pair_scores(2 * i)
        return flash_update(2 * i, s_a, carry[0]), flash_update(2 * i + 1, s_b, carry[1])

    (m_a, l_a, acc_a), (m_b, l_b, acc_b) = lax.fori_loop(0, (c_last + 2) // 2, pair_body, (init, init))
    m_s = jnp.maximum(m_a, m_b)
    w_a, w_b = jnp.exp(m_a - m_s), jnp.exp(m_b - m_s)
    o_slc = (w_a * acc_a + w_b * acc_b) / (w_a * l_a + w_b * l_b)

    gate_ref[...] = jax.nn.sigmoid(g_ref[...].T)
    for h in range(NSA_HPG):
        sl = slice(h * tq, (h + 1) * tq)
        g_c, g_s, g_w = [gate_ref[pl.ds((grp * NSA_HPG + h) * 3 + br, 1), :] for br in range(3)]
        o_h = g_c * o_cmp[:, sl] + g_s * o_slc[:, sl] + g_w * o_win[:, sl]
        o_ref[0, :, h * HEAD_DIM:(h + 1) * HEAD_DIM] = o_h.T.astype(o_ref.dtype)


def nsa_attention(q, k_cmp, v_cmp, ks, kw, z, vs_col, vw_col, gates, t_len):
    b = q.shape[0] // t_len
    tq, ck = ATT_TQ, ATT_CK
    nq = t_len // tq
    n_cmp = k_cmp.shape[2]
    n_slc = t_len // SLC_BLOCK
    assert (t_len // ck) % 2 == 0
    assert vs_col % HEAD_DIM == 0 and vw_col % HEAD_DIM == 0
    seq = lambda a: a.reshape(b, t_len, a.shape[1])
    kv_full = pl.BlockSpec((1, t_len, HEAD_DIM), lambda i, g, j: (i, 0, g))
    vs_full = pl.BlockSpec((1, t_len, HEAD_DIM), lambda i, g, j: (i, 0, vs_col // HEAD_DIM + g))
    vw_full = pl.BlockSpec((1, t_len, HEAD_DIM), lambda i, g, j: (i, 0, vw_col // HEAD_DIM + g))
    cmp_full = pl.BlockSpec((1, 1, n_cmp, HEAD_DIM), lambda i, g, j: (i, g, 0, 0))
    return pl.pallas_call(
        _nsa_kernel,
        grid=(b, NSA_KV_HEADS, nq),
        in_specs=[pl.BlockSpec((1, tq, NSA_HPG * HEAD_DIM), lambda i, g, j: (i, j, g)),
                  cmp_full, cmp_full, kv_full, vs_full, kv_full, vw_full,
                  pl.BlockSpec((tq, LANE), lambda i, g, j: (i * nq + j, 0))],
        out_specs=pl.BlockSpec((1, tq, NSA_HPG * HEAD_DIM), lambda i, g, j: (i, j, g)),
        out_shape=jax.ShapeDtypeStruct((b, t_len, NSA_WIDTH), bf16),
        scratch_shapes=[pltpu.VMEM((tq // LANE, n_cmp, LANE), f32), pltpu.VMEM((n_slc, tq), f32),
                        pltpu.VMEM((n_slc, tq), f32), pltpu.VMEM((LANE, tq), f32)],
        compiler_params=_cparams(("parallel", "parallel", "arbitrary")),
        name="nsa_attention",
    )(seq(q), k_cmp, v_cmp, seq(ks), seq(z), seq(kw), seq(z), gates)


def _pool_kernel(x_ref, halo_ref, w_ref, scale_ref, o_ref, *, t_len):
    tm = x_ref.shape[0]
    i = pl.program_id(0)
    t_start = (i * tm) % t_len
    halo = jnp.where(t_start == 0, 0.0, halo_ref[...])
    tpos = t_start + lax.broadcasted_iota(jnp.int32, (tm, 1), 0)
    kw = w_ref.shape[1]
    for gi, w_len in enumerate(POOL_WINDOWS):
        sl = slice(gi * POOL_GROUP, (gi + 1) * POOL_GROUP)
        xs = slice(gi * POOL_GROUP, gi * POOL_GROUP + kw)
        x = x_ref[:, xs]
        acc = jnp.concatenate([halo[:, xs], x], axis=0)
        first = -POOL_HALO
        span = 1
        while span < w_len:
            acc = acc[span:, :] + acc[:-span, :]
            first += span
            span *= 2
        wsum = acc[-first:-first + tm, :]
        count = jnp.minimum(tpos + 1, w_len).astype(f32)
        y = (wsum / count - x).astype(bf16)
        y = jnp.dot(y, w_ref[gi].astype(bf16), preferred_element_type=f32)
        o_ref[:, sl] = (y * scale_ref[:, sl]).astype(o_ref.dtype)


def pool_mixer(x, w, scale, t_len, lane0=0, tm=256):
    m = x.shape[0]
    groups, cg, _ = w.shape
    c = groups * cg
    kw = -(-(lane0 + cg) // LANE) * LANE
    xc = (groups - 1) * cg + kw
    assert lane0 < LANE and xc <= x.shape[1]
    w_shift = jnp.pad(w, ((0, 0), (lane0, kw - lane0 - cg), (0, 0)))
    hb = tm // POOL_HALO
    return pl.pallas_call(
        functools.partial(_pool_kernel, t_len=t_len),
        grid=(m // tm,),
        in_specs=[pl.BlockSpec((tm, xc), lambda i: (i, 0)),
                  pl.BlockSpec((POOL_HALO, xc), lambda i: (jnp.maximum(i * hb - 1, 0), 0)),
                  pl.BlockSpec(w_shift.shape, lambda i: (0, 0, 0)),
                  pl.BlockSpec((1, c), lambda i: (0, 0))],
        out_specs=pl.BlockSpec((tm, c), lambda i: (i, 0)),
        out_shape=jax.ShapeDtypeStruct((m, c), bf16),
        compiler_params=_cparams(("parallel",)),
        name="pool_mixer",
    )(x, x, w_shift, scale.reshape(1, c))


def _gmlp_kernel(u_ref, v_ref, lg_ref, lb_ref, ws_ref, bs_ref, o_ref, vn_ref):
    tc = v_ref.shape[0]
    vf = v_ref[...].astype(f32)
    mu = jnp.mean(vf, axis=-1, keepdims=True)
    d = vf - mu
    var = jnp.mean(d * d, axis=-1, keepdims=True)
    vn_ref[...] = (d * lax.rsqrt(var + EPS) * lg_ref[...] + lb_ref[...]).astype(bf16)
    gd = v_ref.shape[1] // GMLP_GROUPS
    tri = lax.broadcasted_iota(jnp.int32, (tc, tc), 0) >= lax.broadcasted_iota(jnp.int32, (tc, tc), 1)
    for g in range(GMLP_GROUPS):
        sl = slice(g * gd, (g + 1) * gd)
        w = jnp.where(tri, ws_ref[g], 0.0).astype(bf16)
        s = jnp.dot(w, vn_ref[:, sl], preferred_element_type=f32) + bs_ref[:, g:g + 1]
        o_ref[:, sl] = (u_ref[:, sl].astype(f32) * s).astype(o_ref.dtype)


def gmlp_gate(z, ln_g, ln_b, ws, bs):
    m, two_w = z.shape
    width = two_w // 2
    tc = GMLP_CHUNK
    row = pl.BlockSpec((1, width), lambda i: (0, 0))
    return pl.pallas_call(
        _gmlp_kernel,
        grid=(m // tc,),
        in_specs=[pl.BlockSpec((tc, width), lambda i: (i, 0)),
                  pl.BlockSpec((tc, width), lambda i: (i, 1)),
                  row, row,
                  pl.BlockSpec(ws.shape, lambda i: (0, 0, 0)),
                  pl.BlockSpec((tc, GMLP_GROUPS), lambda i: (0, 0))],
        out_specs=pl.BlockSpec((tc, width), lambda i: (i, 0)),
        out_shape=jax.ShapeDtypeStruct((m, width), bf16),
        scratch_shapes=[pltpu.VMEM((tc, width), bf16)],
        compiler_params=_cparams(("parallel",)),
        name="gmlp_gate",
    )(z, z, ln_g.reshape(1, width), ln_b.reshape(1, width), ws, bs.T)


def _ffn(x, norm_g, wg, wu, wd, layer):
    h = rms_norm(x, norm_g, bf16)
    gu, wd_bf = matmul_ws(h, (wg, wu), layer, wg.shape[2], out_dtype=bf16, bn=FFN_PAD,
                          side=wd, side_layer=layer)
    return matmul(gu, wd_bf, res=x)


def _even_layer(x, b, t_len, norm_g, w_in, j, pe_k, pe_v, w1_k, w2_k, w1_v, w2_v, pool_w, pool_scale, w_out):
    m = x.shape[0]
    h = rms_norm(x, norm_g, bf16)
    qkv_w = NSA_WIDTH + 6 * KV_WIDTH
    w_in_t = jnp.swapaxes(w_in, 1, 2)
    z = matmul_ws(h, (w_in_t,), j, qkv_w, out_dtype=bf16, w_transposed=True)
    zgp = matmul_ws(h, (w_in_t,), j, w_in.shape[2] - qkv_w, col0=qkv_w, bn=512, w_transposed=True)

    q, kc, vc, ks, kw = qkv_post(z, t_len)
    seq = lambda a: a.reshape(b, t_len, a.shape[1])
    k_cmp = compress(seq(kc), pe_k, w1_k, w2_k)
    v_cmp = compress(seq(vc), pe_v, w1_v, w2_v)
    o = nsa_attention(q, k_cmp, v_cmp, ks, kw, z, NSA_WIDTH + 3 * KV_WIDTH, NSA_WIDTH + 5 * KV_WIDTH,
                      zgp, t_len)
    y_pool = pool_mixer(zgp, pool_w, pool_scale, t_len, lane0=N_GATES)
    return matmul_ws((o.reshape(m, NSA_WIDTH), y_pool), (w_out,), j, w_out.shape[2], res=x, bn=512)


def _odd_layer(x, norm_g, w_in, j, ln_g, ln_b, ws, bs, w_out):
    h = rms_norm(x, norm_g, bf16)
    z, w_out_bf = matmul_ws(h, (w_in,), j, w_in.shape[2], act="gelu", out_dtype=bf16,
                            side=w_out, side_layer=j)
    y = gmlp_gate(z, ln_g, ln_b, ws, bs)
    return matmul(y, w_out_bf, res=x)


def kernel(x, norm_mix_even, w_in_even, cmp_pe_k, cmp_pe_v, cmp_w1_k, cmp_w2_k, cmp_w1_v, cmp_w2_v, pool_w, pool_scale, w_out_even, norm_mix_odd, w_in_odd, gmlp_ln_g, gmlp_ln_b, gmlp_ws, gmlp_bs, w_out_odd, norm_ffn, w_ffn_gate, w_ffn_up, w_ffn_down, norm_final):
    b, t_len, d = x.shape
    depth = norm_ffn.shape[0]
    xf = x.reshape(b * t_len, d)
    for layer in range(depth):
        j = layer // 2
        if layer % 2 == 0:
            xf = _even_layer(xf, b, t_len, norm_mix_even[j], w_in_even, j, cmp_pe_k[j], cmp_pe_v[j],
                             cmp_w1_k[j], cmp_w2_k[j], cmp_w1_v[j], cmp_w2_v[j], pool_w[j],
                             pool_scale[j], w_out_even)
        else:
            xf = _odd_layer(xf, norm_mix_odd[j], w_in_odd, j, gmlp_ln_g[j], gmlp_ln_b[j], gmlp_ws[j],
                            gmlp_bs[j], w_out_odd)
        xf = _ffn(xf, norm_ffn[layer], w_ffn_gate, w_ffn_up, w_ffn_down, layer)
    return rms_norm(xf, norm_final, x.dtype).reshape(b, t_len, d)
```

```python
import functools

import jax
import jax.numpy as jnp
from jax import lax
from jax.experimental import pallas as pl
from jax.experimental.pallas import tpu as pltpu

HEAD_DIM = 128
NSA_HEADS = 16
NSA_KV_HEADS = 4
NSA_HPG = NSA_HEADS // NSA_KV_HEADS
NSA_WIDTH = NSA_HEADS * HEAD_DIM
KV_WIDTH = NSA_KV_HEADS * HEAD_DIM
CMP_STRIDE = 16
CMP_BLOCK = 32
CMP_HIDDEN = 256
SLC_BLOCK = 64
SLC_TOPK = 16
WINDOW = 512
N_GATES = 3 * NSA_HEADS
ATTN_SCALE = HEAD_DIM ** -0.5
ROPE_THETA = 500000.0
ROPE_DIM = HEAD_DIM // 4
POOL_WINDOWS = (2, 4, 8, 16)
POOL_GROUP = 512
POOL_HALO = 16
GMLP_CHUNK = 128
GMLP_GROUPS = 16
EPS = 1e-6
NEG = -1e30

LANE = 128
MXU_COLS = 256
VMEM_LIMIT_BYTES = 56 * 1024 * 1024
ATT_TQ = 256
ATT_CK = 256
FFN_PAD = 512
MM_VMEM_BUDGET_BYTES = 48 * 1024 * 1024

bf16 = jnp.bfloat16
f32 = jnp.float32


def _cparams(sem):
    return pltpu.CompilerParams(dimension_semantics=sem, vmem_limit_bytes=VMEM_LIMIT_BYTES)


def _rms_kernel(x_ref, g_ref, o_ref):
    x = x_ref[...]
    ms = jnp.mean(x * x, axis=-1, keepdims=True)
    o_ref[...] = (x * lax.rsqrt(ms + EPS) * g_ref[...]).astype(o_ref.dtype)


def rms_norm(x, g, out_dtype, tm=512):
    m, d = x.shape
    return pl.pallas_call(
        _rms_kernel,
        grid=(m // tm,),
        in_specs=[pl.BlockSpec((tm, d), lambda i: (i, 0)), pl.BlockSpec((1, d), lambda i: (0, 0))],
        out_specs=pl.BlockSpec((tm, d), lambda i: (i, 0)),
        out_shape=jax.ShapeDtypeStruct((m, d), out_dtype),
        compiler_params=_cparams(("parallel",)),
        name="rms_norm",
    )(x, g.reshape(1, d))


def _gelu_tanh(x):
    c0 = 2.0 * 0.7978845608028654
    return x * jax.nn.sigmoid(x * (c0 + (c0 * 0.044715) * (x * x)))


def _mm_kernel(*refs, nk, has_res, act):
    if has_res:
        a_ref, w_ref, r_ref, o_ref = refs[:4]
        scratch = refs[4:]
    else:
        a_ref, w_ref, o_ref = refs[:3]
        r_ref = None
        scratch = refs[3:]

    bn = o_ref.shape[1]
    sub = min(bn, MXU_COLS)

    def step(first, last):
        for s in range(bn // sub):
            cs = slice(s * sub, (s + 1) * sub)
            d = jnp.dot(a_ref[...], w_ref[:, cs], preferred_element_type=f32)
            if not first:
                d = scratch[0][:, cs] + d
            if not last:
                scratch[0][:, cs] = d
                continue
            if act == "gelu":
                d = _gelu_tanh(d)
            if has_res:
                d = d + r_ref[:, cs]
            o_ref[:, cs] = d.astype(o_ref.dtype)

    if nk == 1:
        step(True, True)
        return
    k = pl.program_id(2)
    pl.when(k == 0)(functools.partial(step, True, False))
    pl.when(jnp.logical_and(k > 0, k < nk - 1))(functools.partial(step, False, False))
    pl.when(k == nk - 1)(functools.partial(step, False, True))


def _pick_bk(kdim, bm, bn, out_bytes, has_res):
    fixed = bm * bn * (4 + 2 * out_bytes + (8 if has_res else 0))
    for nk in range(1, kdim // LANE + 1):
        bk = kdim // nk
        acc = bm * bn * 4 if nk > 1 else 0
        if kdim % nk == 0 and bk % LANE == 0 and fixed + acc + 4 * bk * (bm + bn) <= MM_VMEM_BUDGET_BYTES:
            return bk
    raise ValueError("no K tile fits")


def matmul(a, w, res=None, act=None, out_dtype=f32, bm=1024, bn=1024):
    m, kdim = a.shape
    _, n = w.shape
    bn = min(bn, n)
    bk = _pick_bk(kdim, bm, bn, jnp.dtype(out_dtype).itemsize, res is not None)
    assert m % bm == 0 and n % bn == 0 and kdim % bk == 0
    nk = kdim // bk
    in_specs = [pl.BlockSpec((bm, bk), lambda j, i, k: (i, k)),
                pl.BlockSpec((bk, bn), lambda j, i, k: (k, j))]
    args = [a, w]
    if res is not None:
        in_specs.append(pl.BlockSpec((bm, bn), lambda j, i, k: (i, j)))
        args.append(res)
    scratch = [pltpu.VMEM((bm, bn), f32)] if nk > 1 else []
    return pl.pallas_call(
        functools.partial(_mm_kernel, nk=nk, has_res=res is not None, act=act),
        grid=(n // bn, m // bm, nk),
        in_specs=in_specs,
        out_specs=pl.BlockSpec((bm, bn), lambda j, i, k: (i, j)),
        out_shape=jax.ShapeDtypeStruct((m, n), out_dtype),
        scratch_shapes=scratch,
        compiler_params=_cparams(("parallel", "parallel", "arbitrary")),
        name="matmul",
    )(*args)


def _mm_ws_kernel(*refs, na, nw, nt, nm, ck, bn, n_valid, act, has_res, has_side, side_rows,
                  w_transposed):
    refs = list(refs)
    a_refs = [refs.pop(0) for _ in range(na)]
    w_refs = [refs.pop(0) for _ in range(nw)]
    r_ref = refs.pop(0) if has_res else None
    s_ref = refs.pop(0) if has_side else None
    o_ref = refs.pop(0)
    so_ref = refs.pop(0) if has_side else None
    wbf_ref = refs.pop(0)
    j = pl.program_id(0)
    i = pl.program_id(1)

    def cast(edge):
        for t in range(nw):
            w = w_refs[t][...]
            if edge:
                col = j * bn + lax.broadcasted_iota(jnp.int32, w.shape, 0 if w_transposed else 1)
                w = jnp.where(col < n_valid, w, 0.0)
            if w_transposed:
                w = w.T
            wbf_ref[(j % 2) * nw + t, pl.ds(pl.multiple_of(i * ck, ck), ck), :] = w.astype(bf16)

    if n_valid % bn:
        pl.when(j < nt - 1)(functools.partial(cast, False))
        pl.when(j == nt - 1)(functools.partial(cast, True))
    else:
        pl.when(j < nt)(functools.partial(cast, False))

    @pl.when(j == 0)
    def _():
        o_ref[...] = jnp.zeros(o_ref.shape, o_ref.dtype)

    @pl.when(j > 0)
    def _():
        slot = ((j - 1) % 2) * nw
        sub = min(bn, MXU_COLS)

        def lhs_dot(w_slot, cs):
            k0, acc = 0, None
            for a_ref in a_refs:
                kp = a_ref.shape[1]
                part = jnp.dot(a_ref[...], wbf_ref[w_slot, k0:k0 + kp, cs], preferred_element_type=f32)
                acc = part if acc is None else acc + part
                k0 += kp
            return acc

        for s in range(bn // sub):
            cs = slice(s * sub, (s + 1) * sub)
            d = lhs_dot(slot, cs)
            if nw == 2:
                d = d * jax.nn.sigmoid(d) * lhs_dot(slot + 1, cs)
            if act == "gelu":
                d = _gelu_tanh(d)
            if has_res:
                d = d + r_ref[:, cs]
            o_ref[:, cs] = d.astype(o_ref.dtype)

    if has_side:
        rs = s_ref.shape[0]
        slab = jnp.maximum(j - 1, 0) * nm + i * jnp.minimum(j, 1)
        full = slab < side_rows // rs

        @pl.when(full)
        def _():
            so_ref[...] = s_ref[...].astype(so_ref.dtype)

        @pl.when(jnp.logical_not(full))
        def _():
            row = slab * rs + lax.broadcasted_iota(jnp.int32, s_ref.shape, 0)
            so_ref[...] = jnp.where(row < side_rows, s_ref[...], 0.0).astype(so_ref.dtype)


def matmul_ws(a, ws, layer, n_cols, col0=0, res=None, act=None, out_dtype=f32, bm=1024, bn=1024,
              side=None, side_layer=0, w_transposed=False):
    a_parts = a if isinstance(a, (tuple, list)) else (a,)
    m = a_parts[0].shape[0]
    kdim = sum(p.shape[1] for p in a_parts)
    nw = len(ws)
    nm = m // bm
    ck = kdim // nm
    nt = pl.cdiv(n_cols, bn)
    ct0 = col0 // bn
    assert m % bm == 0 and kdim % nm == 0 and ck % 8 == 0 and col0 % bn == 0
    a_specs = [pl.BlockSpec((bm, p.shape[1]), lambda j, i: (i * jnp.minimum(j, 1), 0)) for p in a_parts]
    if w_transposed:
        w_spec = pl.BlockSpec((None, bn, ck), lambda j, i: (layer, ct0 + jnp.minimum(j, nt - 1), i))
    else:
        w_spec = pl.BlockSpec((None, ck, bn), lambda j, i: (layer, i, ct0 + jnp.minimum(j, nt - 1)))
    o_map = lambda j, i: (i * jnp.minimum(j, 1), jnp.maximum(j - 1, 0))
    in_specs = a_specs + [w_spec] * nw
    args = list(a_parts) + list(ws)
    if res is not None:
        in_specs.append(pl.BlockSpec((bm, bn), o_map))
        args.append(res)
    out_specs = [pl.BlockSpec((bm, bn), o_map)]
    out_shape = [jax.ShapeDtypeStruct((m, nt * bn), out_dtype)]
    side_rows = 0
    if side is not None:
        _, side_rows, side_cols = side.shape
        n_slabs = nt * nm
        rs = -(-side_rows // n_slabs)
        rs = -(-rs // 16) * 16
        last = side_rows // rs - (1 if side_rows % rs == 0 else 0)
        s_map = lambda j, i: (jnp.maximum(j - 1, 0) * nm + i * jnp.minimum(j, 1), 0)
        in_specs.append(pl.BlockSpec((None, rs, side_cols),
                                     lambda j, i: (side_layer, jnp.minimum(s_map(j, i)[0], last), 0)))
        args.append(side)
        out_specs.append(pl.BlockSpec((rs, side_cols), s_map))
        out_shape.append(jax.ShapeDtypeStruct((n_slabs * rs, side_cols), bf16))
    outs = pl.pallas_call(
        functools.partial(_mm_ws_kernel, na=len(a_parts), nw=nw, nt=nt, nm=nm, ck=ck, bn=bn,
                          n_valid=n_cols, act=act,
                          has_res=res is not None, has_side=side is not None, side_rows=side_rows,
                          w_transposed=w_transposed),
        grid=(nt + 1, nm),
        in_specs=in_specs,
        out_specs=out_specs,
        out_shape=out_shape,
        scratch_shapes=[pltpu.VMEM((2 * nw, kdim, bn), bf16)],
        compiler_params=_cparams(("arbitrary", "arbitrary")),
        name="matmul_ws",
    )(*args)
    return outs if side is not None else outs[0]


def _rope(x, c, s_lo, s_hi):
    half = ROPE_DIM // 2
    return x * c + pltpu.roll(x, HEAD_DIM - half, 1) * s_lo + pltpu.roll(x, half, 1) * s_hi


def _qkv_post_kernel(z_ref, c_ref, slo_ref, shi_ref, q_ref, kc_ref, vc_ref, ks_ref, kw_ref):
    c, s_lo, s_hi = c_ref[...], slo_ref[...], shi_ref[...]
    for h in range(NSA_HEADS):
        sl = slice(h * HEAD_DIM, (h + 1) * HEAD_DIM)
        q_ref[:, sl] = (_rope(z_ref[:, sl].astype(f32), c, s_lo, s_hi) * ATTN_SCALE).astype(q_ref.dtype)
    for idx, o_ref, roped in ((0, kc_ref, True), (1, vc_ref, False), (2, ks_ref, True), (4, kw_ref, True)):
        for g in range(NSA_KV_HEADS):
            col = NSA_WIDTH + idx * KV_WIDTH + g * HEAD_DIM
            x = z_ref[:, col:col + HEAD_DIM].astype(f32)
            if roped:
                x = _rope(x, c, s_lo, s_hi)
            o_ref[:, g * HEAD_DIM:(g + 1) * HEAD_DIM] = x.astype(o_ref.dtype)


def _rope_tables(t_len):
    half = ROPE_DIM // 2
    inv_freq = ROPE_THETA ** (-jnp.arange(half, dtype=f32) * 2.0 / ROPE_DIM)
    ang = jnp.arange(t_len).astype(f32)[:, None] * inv_freq[None, :]
    cos, sin = jnp.cos(ang), jnp.sin(ang)
    ones = jnp.ones((t_len, HEAD_DIM - ROPE_DIM), f32)
    zeros = jnp.zeros((t_len, HEAD_DIM - ROPE_DIM), f32)
    zh = jnp.zeros((t_len, half), f32)
    c = jnp.concatenate([cos, cos, ones], axis=1)
    s_lo = jnp.concatenate([-sin, zh, zeros], axis=1)
    s_hi = jnp.concatenate([zh, sin, zeros], axis=1)
    return c, s_lo, s_hi


def qkv_post(z, t_len, tm=256):
    m = z.shape[0]
    c, s_lo, s_hi = _rope_tables(t_len)
    nt = t_len // tm
    tab = pl.BlockSpec((tm, HEAD_DIM), lambda i: (i % nt, 0))
    kv = pl.BlockSpec((tm, KV_WIDTH), lambda i: (i, 0))
    shp = lambda dt: jax.ShapeDtypeStruct((m, KV_WIDTH), dt)
    return pl.pallas_call(
        _qkv_post_kernel,
        grid=(m // tm,),
        in_specs=[pl.BlockSpec((tm, z.shape[1]), lambda i: (i, 0)), tab, tab, tab],
        out_specs=[pl.BlockSpec((tm, NSA_WIDTH), lambda i: (i, 0)), kv, kv, kv, kv],
        out_shape=[jax.ShapeDtypeStruct((m, NSA_WIDTH), bf16), shp(f32), shp(f32), shp(bf16), shp(bf16)],
        compiler_params=_cparams(("parallel",)),
        name="qkv_post",
    )(z, c, s_lo, s_hi)


def _compress_kernel(a_ref, pe_ref, w1_ref, w2_ref, o_ref):
    n_chunks = a_ref.shape[1] // CMP_STRIDE
    top = bot = None
    for l in range(CMP_STRIDE):
        rows = a_ref[0, pl.ds(l, n_chunks, stride=CMP_STRIDE), :]
        t = jnp.dot((rows + pe_ref[l:l + 1, :]).astype(bf16), w1_ref[l].astype(bf16),
                    preferred_element_type=f32)
        u = jnp.dot((rows + pe_ref[CMP_STRIDE + l:CMP_STRIDE + l + 1, :]).astype(bf16),
                    w1_ref[CMP_STRIDE + l].astype(bf16), preferred_element_type=f32)
        top = t if top is None else top + t
        bot = u if bot is None else bot + u
    pre = top + pltpu.roll(bot, n_chunks - 1, 0)
    hid = _gelu_tanh(pre)
    o_ref[0, 0] = jnp.dot(hid.astype(bf16), w2_ref[...].astype(bf16),
                          preferred_element_type=f32).astype(o_ref.dtype)


def compress(a, pe, w1, w2):
    b, t_len, width = a.shape
    n_chunks = t_len // CMP_STRIDE
    return pl.pallas_call(
        _compress_kernel,
        grid=(b, width // HEAD_DIM),
        in_specs=[pl.BlockSpec((1, t_len, HEAD_DIM), lambda i, j: (i, 0, j)),
                  pl.BlockSpec(pe.shape, lambda i, j: (0, 0)),
                  pl.BlockSpec(w1.shape, lambda i, j: (0, 0, 0)),
                  pl.BlockSpec(w2.shape, lambda i, j: (0, 0))],
        out_specs=pl.BlockSpec((1, 1, n_chunks, HEAD_DIM), lambda i, j: (i, j, 0, 0)),
        out_shape=jax.ShapeDtypeStruct((b, width // HEAD_DIM, n_chunks, HEAD_DIM), bf16),
        compiler_params=_cparams(("parallel", "parallel")),
        name="compress",
    )(a, pe, w1, w2)


def _nsa_kernel(q_ref, kc_ref, vc_ref, ks_ref, vs_ref, kw_ref, vw_ref, g_ref, o_ref,
                imp_ref, score_ref, sel_ref, gate_ref):
    tq, ck = ATT_TQ, ATT_CK
    nl = NSA_HPG * tq
    grp = pl.program_id(1)
    qi = pl.program_id(2)
    t0 = qi * tq
    qt = jnp.concatenate([q_ref[0, :, h * HEAD_DIM:(h + 1) * HEAD_DIM].T for h in range(NSA_HPG)],
                         axis=1)
    n_cmp = kc_ref.shape[2]
    n_slc = sel_ref.shape[0]

    def pv(v, pr):
        return lax.dot_general(v, pr.astype(bf16), (((0,), (0,)), ((), ())), preferred_element_type=f32)

    def v_chunk(v_ref, c):
        return v_ref[0, pl.ds(pl.multiple_of(c * ck, ck), ck), :]

    kpos0 = lax.broadcasted_iota(jnp.int32, (ck, tq), 0)
    qpos = t0 + lax.broadcasted_iota(jnp.int32, (ck, tq), 1)
    c_last = (t0 + tq - 1) // ck

    def raw_scores(k_ref, c):
        k = k_ref[0, pl.ds(pl.multiple_of(c * ck, ck), ck), :]
        return jnp.dot(k, qt, preferred_element_type=f32)

    def masked(sc, mask):
        return jnp.where(jnp.concatenate([mask] * NSA_HPG, axis=1), sc, NEG)

    s = jnp.dot(kc_ref[0, 0], qt, preferred_element_type=f32)
    n_win = -(-tq // ck) + -(-(WINDOW - 1) // ck)
    c_win = [c_last - back for back in range(n_win)]
    s_win = [raw_scores(kw_ref, jnp.maximum(c, 0)) for c in c_win]

    def pair_scores(c):
        return raw_scores(ks_ref, c), raw_scores(ks_ref, c + 1)

    blk_end = lax.broadcasted_iota(jnp.int32, (n_cmp, nl), 0) * CMP_STRIDE + (CMP_BLOCK - 1)
    qpos4 = t0 + lax.broadcasted_iota(jnp.int32, (n_cmp, nl), 1) % tq
    valid = blk_end <= qpos4
    s = jnp.where(valid, s, NEG)
    mx = jnp.max(s, axis=0, keepdims=True)
    e = jnp.exp(s - mx)
    den = jnp.sum(e, axis=0, keepdims=True)
    p = e * jnp.where(mx > 0.5 * NEG, 1.0 / den, 0.0)
    o_cmp = pv(vc_ref[0, 0], p)
    imp = (p[:, 0:tq] + p[:, tq:2 * tq]) + p[:, 2 * tq:3 * tq] + p[:, 3 * tq:4 * tq]
    for lt in range(tq // LANE):
        imp_ref[lt] = imp[:, lt * LANE:(lt + 1) * LANE]

    for i, c in enumerate(c_win):
        kpos = kpos0 + c * ck
        s_win[i] = masked(s_win[i], (kpos <= qpos) & (kpos > qpos - WINDOW) & (kpos >= 0))
    m_w = functools.reduce(jnp.maximum, [jnp.max(sw, axis=0, keepdims=True) for sw in s_win])
    p_win = [jnp.exp(sw - m_w) for sw in s_win]
    l_w = functools.reduce(jnp.add, [jnp.sum(pw, axis=0, keepdims=True) for pw in p_win])
    acc_w = functools.reduce(jnp.add, [pv(v_chunk(vw_ref, jnp.maximum(c, 0)), pw)
                                       for c, pw in zip(c_win, p_win)])
    o_win = acc_w * (1.0 / l_w)

    r = SLC_BLOCK // CMP_STRIDE
    rows = [jnp.concatenate([imp_ref[lt, pl.ds(k, n_slc, stride=r), :] for lt in range(tq // LANE)],
                            axis=1) for k in range(r)]
    blk = lax.broadcasted_iota(jnp.int32, (n_slc, tq), 0)
    prev = jnp.where(blk == 0, 0.0, pltpu.roll(rows[r - 1], 1, 0))
    score = (rows[0] + rows[1] + rows[2]) + 0.5 * rows[r - 1] + 0.5 * prev
    cur = (t0 + lax.broadcasted_iota(jnp.int32, (n_slc, tq), 1)) // SLC_BLOCK
    score = jnp.where((blk == cur) | (blk == 0), -NEG, jnp.where(blk > cur, NEG, score))
    score_ref[...] = score
    sub = lax.broadcasted_iota(jnp.int32, (8, tq), 0)
    groups = [score[8 * gi:8 * gi + 8, :] for gi in range(n_slc // 8)]
    cnt = [jnp.zeros((8, tq), f32) for _ in groups]
    for jp in range(n_slc):
        row = jnp.broadcast_to(score_ref[pl.ds(jp, 1), :], (8, tq))
        for gi, sc_g in enumerate(groups):
            if jp < 8 * gi:
                beats = row >= sc_g
            elif jp >= 8 * gi + 8:
                beats = row > sc_g
            else:
                beats = (row > sc_g) | ((row == sc_g) & (sub > jp - 8 * gi))
            cnt[gi] = cnt[gi] + jnp.where(beats, 1.0, 0.0)
    sel_ref[...] = jnp.where(jnp.concatenate(cnt, axis=0) < min(SLC_TOPK, n_slc), 1.0, 0.0)

    init = (jnp.full((1, nl), NEG, f32), jnp.zeros((1, nl), f32), jnp.zeros((HEAD_DIM, nl), f32))

    def flash_update(c, sc, carry):
        m, l, acc = carry
        per_chunk = ck // SLC_BLOCK
        selrows = jnp.concatenate(
            [jnp.broadcast_to(sel_ref[pl.ds(c * per_chunk + i, 1), :], (SLC_BLOCK, tq))
             for i in range(per_chunk)], axis=0)
        sc = masked(sc, (selrows > 0.5) & (kpos0 + c * ck <= qpos))
        m_new = jnp.maximum(m, jnp.max(sc, axis=0, keepdims=True))
        alpha = jnp.exp(m - m_new)
        pr = jnp.exp(sc - m_new)
        l = alpha * l + jnp.sum(pr, axis=0, keepdims=True)
        acc = alpha * acc + pv(v_chunk(vs_ref, c), pr)
        return m_new, l, acc

    def pair_body(i, carry):
        s_a, s_b = pair_scores(2 * i)
        return flash_update(2 * i, s_a, carry[0]), flash_update(2 * i + 1, s_b, carry[1])

    (m_a, l_a, acc_a), (m_b, l_b, acc_b) = lax.fori_loop(0, (c_last + 2) // 2, pair_body, (init, init))
    m_s = jnp.maximum(m_a, m_b)
    w_a, w_b = jnp.exp(m_a - m_s), jnp.exp(m_b - m_s)
    r_s = 1.0 / (w_a * l_a + w_b * l_b)
    o_slc = (w_a * r_s) * acc_a + (w_b * r_s) * acc_b

    gate_ref[...] = jax.nn.sigmoid(g_ref[...].T)
    for h in range(NSA_HPG):
        sl = slice(h * tq, (h + 1) * tq)
        g_c, g_s, g_w = [gate_ref[pl.ds((grp * NSA_HPG + h) * 3 + br, 1), :] for br in range(3)]
        o_h = g_c * o_cmp[:, sl] + g_s * o_slc[:, sl] + g_w * o_win[:, sl]
        o_ref[0, :, h * HEAD_DIM:(h + 1) * HEAD_DIM] = o_h.T.astype(o_ref.dtype)


def nsa_attention(q, k_cmp, v_cmp, ks, kw, z, vs_col, vw_col, gates, t_len):
    b = q.shape[0] // t_len
    tq, ck = ATT_TQ, ATT_CK
    nq = t_len // tq
    n_cmp = k_cmp.shape[2]
    n_slc = t_len // SLC_BLOCK
    assert (t_len // ck) % 2 == 0
    assert vs_col % HEAD_DIM == 0 and vw_col % HEAD_DIM == 0
    seq = lambda a: a.reshape(b, t_len, a.shape[1])
    kv_full = pl.BlockSpec((1, t_len, HEAD_DIM), lambda i, g, j: (i, 0, g))
    vs_full = pl.BlockSpec((1, t_len, HEAD_DIM), lambda i, g, j: (i, 0, vs_col // HEAD_DIM + g))
    vw_full = pl.BlockSpec((1, t_len, HEAD_DIM), lambda i, g, j: (i, 0, vw_col // HEAD_DIM + g))
    cmp_full = pl.BlockSpec((1, 1, n_cmp, HEAD_DIM), lambda i, g, j: (i, g, 0, 0))
    return pl.pallas_call(
        _nsa_kernel,
        grid=(b, NSA_KV_HEADS, nq),
        in_specs=[pl.BlockSpec((1, tq, NSA_HPG * HEAD_DIM), lambda i, g, j: (i, j, g)),
                  cmp_full, cmp_full, kv_full, vs_full, kv_full, vw_full,
                  pl.BlockSpec((tq, LANE), lambda i, g, j: (i * nq + j, 0))],
        out_specs=pl.BlockSpec((1, tq, NSA_HPG * HEAD_DIM), lambda i, g, j: (i, j, g)),
        out_shape=jax.ShapeDtypeStruct((b, t_len, NSA_WIDTH), bf16),
        scratch_shapes=[pltpu.VMEM((tq // LANE, n_cmp, LANE), f32), pltpu.VMEM((n_slc, tq), f32),
                        pltpu.VMEM((n_slc, tq), f32), pltpu.VMEM((LANE, tq), f32)],
        compiler_params=_cparams(("parallel", "parallel", "arbitrary")),
        name="nsa_attention",
    )(seq(q), k_cmp, v_cmp, seq(ks), seq(z), seq(kw), seq(z), gates)


def _pool_kernel(x_ref, halo_ref, w_ref, scale_ref, o_ref, *, t_len):
    tm = x_ref.shape[0]
    i = pl.program_id(0)
    t_start = (i * tm) % t_len
    halo = jnp.where(t_start == 0, 0.0, halo_ref[...])
    tpos = t_start + lax.broadcasted_iota(jnp.int32, (tm, 1), 0)
    kw = w_ref.shape[1]
    for gi, w_len in enumerate(POOL_WINDOWS):
        sl = slice(gi * POOL_GROUP, (gi + 1) * POOL_GROUP)
        xs = slice(gi * POOL_GROUP, gi * POOL_GROUP + kw)
        x = x_ref[:, xs]
        acc = jnp.concatenate([halo[:, xs], x], axis=0)
        first = -POOL_HALO
        span = 1
        while span < w_len:
            acc = acc[span:, :] + acc[:-span, :]
            first += span
            span *= 2
        wsum = acc[-first:-first + tm, :]
        count = jnp.minimum(tpos + 1, w_len).astype(f32)
        y = (wsum / count - x).astype(bf16)
        y = jnp.dot(y, w_ref[gi].astype(bf16), preferred_element_type=f32)
        o_ref[:, sl] = (y * scale_ref[:, sl]).astype(o_ref.dtype)


def pool_mixer(x, w, scale, t_len, lane0=0, tm=256):
    m = x.shape[0]
    groups, cg, _ = w.shape
    c = groups * cg
    kw = -(-(lane0 + cg) // LANE) * LANE
    xc = (groups - 1) * cg + kw
    assert lane0 < LANE and xc <= x.shape[1]
    w_shift = jnp.pad(w, ((0, 0), (lane0, kw - lane0 - cg), (0, 0)))
    hb = tm // POOL_HALO
    return pl.pallas_call(
        functools.partial(_pool_kernel, t_len=t_len),
        grid=(m // tm,),
        in_specs=[pl.BlockSpec((tm, xc), lambda i: (i, 0)),
                  pl.BlockSpec((POOL_HALO, xc), lambda i: (jnp.maximum(i * hb - 1, 0), 0)),
                  pl.BlockSpec(w_shift.shape, lambda i: (0, 0, 0)),
                  pl.BlockSpec((1, c), lambda i: (0, 0))],
        out_specs=pl.BlockSpec((tm, c), lambda i: (i, 0)),
        out_shape=jax.ShapeDtypeStruct((m, c), bf16),
        compiler_params=_cparams(("parallel",)),
        name="pool_mixer",
    )(x, x, w_shift, scale.reshape(1, c))


def _gmlp_kernel(u_ref, v_ref, lg_ref, lb_ref, ws_ref, bs_ref, o_ref, vn_ref):
    tc = v_ref.shape[0]
    vf = v_ref[...].astype(f32)
    mu = jnp.mean(vf, axis=-1, keepdims=True)
    d = vf - mu
    var = jnp.mean(d * d, axis=-1, keepdims=True)
    vn_ref[...] = (d * lax.rsqrt(var + EPS) * lg_ref[...] + lb_ref[...]).astype(bf16)
    gd = v_ref.shape[1] // GMLP_GROUPS
    tri = lax.broadcasted_iota(jnp.int32, (tc, tc), 0) >= lax.broadcasted_iota(jnp.int32, (tc, tc), 1)
    for g in range(GMLP_GROUPS):
        sl = slice(g * gd, (g + 1) * gd)
        w = jnp.where(tri, ws_ref[g], 0.0).astype(bf16)
        s = jnp.dot(w, vn_ref[:, sl], preferred_element_type=f32) + bs_ref[:, g:g + 1]
        o_ref[:, sl] = (u_ref[:, sl].astype(f32) * s).astype(o_ref.dtype)


def gmlp_gate(z, ln_g, ln_b, ws, bs):
    m, two_w = z.shape
    width = two_w // 2
    tc = GMLP_CHUNK
    row = pl.BlockSpec((1, width), lambda i: (0, 0))
    return pl.pallas_call(
        _gmlp_kernel,
        grid=(m // tc,),
        in_specs=[pl.BlockSpec((tc, width), lambda i: (i, 0)),
                  pl.BlockSpec((tc, width), lambda i: (i, 1)),
                  row, row,
                  pl.BlockSpec(ws.shape, lambda i: (0, 0, 0)),
                  pl.BlockSpec((tc, GMLP_GROUPS), lambda i: (0, 0))],
        out_specs=pl.BlockSpec((tc, width), lambda i: (i, 0)),
        out_shape=jax.ShapeDtypeStruct((m, width), bf16),
        scratch_shapes=[pltpu.VMEM((tc, width), bf16)],
        compiler_params=_cparams(("parallel",)),
        name="gmlp_gate",
    )(z, z, ln_g.reshape(1, width), ln_b.reshape(1, width), ws, bs.T)


def _ffn(x, norm_g, wg, wu, wd, layer):
    h = rms_norm(x, norm_g, bf16)
    gu, wd_bf = matmul_ws(h, (wg, wu), layer, wg.shape[2], out_dtype=bf16, bn=FFN_PAD,
                          side=wd, side_layer=layer)
    return matmul(gu, wd_bf, res=x)


def _even_layer(x, b, t_len, norm_g, w_in, j, pe_k, pe_v, w1_k, w2_k, w1_v, w2_v, pool_w, pool_scale, w_out):
    m = x.shape[0]
    h = rms_norm(x, norm_g, bf16)
    qkv_w = NSA_WIDTH + 6 * KV_WIDTH
    w_in_t = jnp.swapaxes(w_in, 1, 2)
    z = matmul_ws(h, (w_in_t,), j, qkv_w, out_dtype=bf16, w_transposed=True)
    zgp = matmul_ws(h, (w_in_t,), j, w_in.shape[2] - qkv_w, col0=qkv_w, bn=512, w_transposed=True)

    q, kc, vc, ks, kw = qkv_post(z, t_len)
    seq = lambda a: a.reshape(b, t_len, a.shape[1])
    k_cmp = compress(seq(kc), pe_k, w1_k, w2_k)
    v_cmp = compress(seq(vc), pe_v, w1_v, w2_v)
    o = nsa_attention(q, k_cmp, v_cmp, ks, kw, z, NSA_WIDTH + 3 * KV_WIDTH, NSA_WIDTH + 5 * KV_WIDTH,
                      zgp, t_len)
    y_pool = pool_mixer(zgp, pool_w, pool_scale, t_len, lane0=N_GATES)
    return matmul_ws((o.reshape(m, NSA_WIDTH), y_pool), (w_out,), j, w_out.shape[2], res=x, bn=512)


def _odd_layer(x, norm_g, w_in, j, ln_g, ln_b, ws, bs, w_out):
    h = rms_norm(x, norm_g, bf16)
    z, w_out_bf = matmul_ws(h, (w_in,), j, w_in.shape[2], act="gelu", out_dtype=bf16,
                            side=w_out, side_layer=j)
    y = gmlp_gate(z, ln_g, ln_b, ws, bs)
    return matmul(y, w_out_bf, res=x)


def kernel(x, norm_mix_even, w_in_even, cmp_pe_k, cmp_pe_v, cmp_w1_k, cmp_w2_k, cmp_w1_v, cmp_w2_v, pool_w, pool_scale, w_out_even, norm_mix_odd, w_in_odd, gmlp_ln_g, gmlp_ln_b, gmlp_ws, gmlp_bs, w_out_odd, norm_ffn, w_ffn_gate, w_ffn_up, w_ffn_down, norm_final):
    b, t_len, d = x.shape
    depth = norm_ffn.shape[0]
    xf = x.reshape(b * t_len, d)
    for layer in range(depth):
        j = layer // 2
        if layer % 2 == 0:
            xf = _even_layer(xf, b, t_len, norm_mix_even[j], w_in_even, j, cmp_pe_k[j], cmp_pe_v[j],
                             cmp_w1_k[j], cmp_w2_k[j], cmp_w1_v[j], cmp_w2_v[j], pool_w[j],
                             pool_scale[j], w_out_even)
        else:
            xf = _odd_layer(xf, norm_mix_odd[j], w_in_odd, j, gmlp_ln_g[j], gmlp_ln_b[j], gmlp_ws[j],
                            gmlp_bs[j], w_out_odd)
        xf = _ffn(xf, norm_ffn[layer], w_ffn_gate, w_ffn_up, w_ffn_down, layer)
    return rms_norm(xf, norm_final, x.dtype).reshape(b, t_len, d)
```

```python
import functools

import jax
import jax.numpy as jnp
from jax import lax
from jax.experimental import pallas as pl
from jax.experimental.pallas import tpu as pltpu

HEAD_DIM = 128
NSA_HEADS = 16
NSA_KV_HEADS = 4
NSA_HPG = NSA_HEADS // NSA_KV_HEADS
NSA_WIDTH = NSA_HEADS * HEAD_DIM
KV_WIDTH = NSA_KV_HEADS * HEAD_DIM
CMP_STRIDE = 16
CMP_BLOCK = 32
CMP_HIDDEN = 256
SLC_BLOCK = 64
SLC_TOPK = 16
WINDOW = 512
N_GATES = 3 * NSA_HEADS
ATTN_SCALE = HEAD_DIM ** -0.5
ROPE_THETA = 500000.0
ROPE_DIM = HEAD_DIM // 4
POOL_WINDOWS = (2, 4, 8, 16)
POOL_GROUP = 512
POOL_HALO = 16
GMLP_CHUNK = 128
GMLP_GROUPS = 16
EPS = 1e-6
NEG = -1e30

LANE = 128
MXU_COLS = 256
VMEM_LIMIT_BYTES = 56 * 1024 * 1024
ATT_TQ = 256
ATT_CK = 256
FFN_PAD = 512
MM_VMEM_BUDGET_BYTES = 48 * 1024 * 1024

bf16 = jnp.bfloat16
f32 = jnp.float32


def _cparams(sem):
    return pltpu.CompilerParams(dimension_semantics=sem, vmem_limit_bytes=VMEM_LIMIT_BYTES)


def _rms_kernel(x_ref, g_ref, o_ref):
    x = x_ref[...]
    ms = jnp.mean(x * x, axis=-1, keepdims=True)
    o_ref[...] = (x * lax.rsqrt(ms + EPS) * g_ref[...]).astype(o_ref.dtype)


def rms_norm(x, g, out_dtype, tm=512):
    m, d = x.shape
    return pl.pallas_call(
        _rms_kernel,
        grid=(m // tm,),
        in_specs=[pl.BlockSpec((tm, d), lambda i: (i, 0)), pl.BlockSpec((1, d), lambda i: (0, 0))],
        out_specs=pl.BlockSpec((tm, d), lambda i: (i, 0)),
        out_shape=jax.ShapeDtypeStruct((m, d), out_dtype),
        compiler_params=_cparams(("parallel",)),
        name="rms_norm",
    )(x, g.reshape(1, d))


def _gelu_tanh(x):
    c0 = 2.0 * 0.7978845608028654
    return x * jax.nn.sigmoid(x * (c0 + (c0 * 0.044715) * (x * x)))


def _mm_kernel(a_ref, w_ref, r_ref, o_ref, *acc, nk):
    bn = o_ref.shape[1]
    sub = min(bn, MXU_COLS)

    def step(first, last):
        for s in range(bn // sub):
            cs = slice(s * sub, (s + 1) * sub)
            d = jnp.dot(a_ref[...], w_ref[:, cs], preferred_element_type=f32)
            if not first:
                d = acc[0][:, cs] + d
            if last:
                o_ref[:, cs] = d + r_ref[:, cs]
            else:
                acc[0][:, cs] = d

    if nk == 1:
        step(True, True)
        return
    k = pl.program_id(2)
    pl.when(k == 0)(functools.partial(step, True, False))
    pl.when(jnp.logical_and(k > 0, k < nk - 1))(functools.partial(step, False, False))
    pl.when(k == nk - 1)(functools.partial(step, False, True))


def _pick_bk(kdim, bm, bn):
    fixed = bm * bn * (4 + 8 + 8)
    for nk in range(1, kdim // LANE + 1):
        bk = kdim // nk
        acc = bm * bn * 4 if nk > 1 else 0
        if kdim % nk == 0 and bk % LANE == 0 and fixed + acc + 4 * bk * (bm + bn) <= MM_VMEM_BUDGET_BYTES:
            return bk
    raise ValueError("no K tile fits")


def matmul_res(a, w, res, bm=1024, bn=1024):
    m, kdim = a.shape
    _, n = w.shape
    bk = _pick_bk(kdim, bm, bn)
    assert m % bm == 0 and n % bn == 0 and kdim % bk == 0
    nk = kdim // bk
    tile = pl.BlockSpec((bm, bn), lambda j, i, k: (i, j))
    return pl.pallas_call(
        functools.partial(_mm_kernel, nk=nk),
        grid=(n // bn, m // bm, nk),
        in_specs=[pl.BlockSpec((bm, bk), lambda j, i, k: (i, k)),
                  pl.BlockSpec((bk, bn), lambda j, i, k: (k, j)), tile],
        out_specs=tile,
        out_shape=jax.ShapeDtypeStruct((m, n), f32),
        scratch_shapes=[pltpu.VMEM((bm, bn), f32)] if nk > 1 else [],
        compiler_params=_cparams(("parallel", "parallel", "arbitrary")),
        name="matmul_res",
    )(a, w, res)


def _mm_ws_kernel(*refs, na, nw, nt, nm, ck, bn, n_valid, act, has_res, has_side, side_rows,
                  w_transposed):
    refs = list(refs)
    a_refs = [refs.pop(0) for _ in range(na)]
    w_refs = [refs.pop(0) for _ in range(nw)]
    r_ref = refs.pop(0) if has_res else None
    s_ref = refs.pop(0) if has_side else None
    o_ref = refs.pop(0)
    so_ref = refs.pop(0) if has_side else None
    wbf_ref = refs.pop(0)
    j = pl.program_id(0)
    i = pl.program_id(1)

    def cast(edge):
        for t in range(nw):
            w = w_refs[t][...]
            if edge:
                col = j * bn + lax.broadcasted_iota(jnp.int32, w.shape, 0 if w_transposed else 1)
                w = jnp.where(col < n_valid, w, 0.0)
            if w_transposed:
                w = w.T
            wbf_ref[(j % 2) * nw + t, pl.ds(pl.multiple_of(i * ck, ck), ck), :] = w.astype(bf16)

    if n_valid % bn:
        pl.when(j < nt - 1)(functools.partial(cast, False))
        pl.when(j == nt - 1)(functools.partial(cast, True))
    else:
        pl.when(j < nt)(functools.partial(cast, False))

    @pl.when(j == 0)
    def _():
        o_ref[...] = jnp.zeros(o_ref.shape, o_ref.dtype)

    @pl.when(j > 0)
    def _():
        slot = ((j - 1) % 2) * nw
        sub = min(bn, MXU_COLS)

        def lhs_dot(w_slot, cs):
            k0, acc = 0, None
            for a_ref in a_refs:
                kp = a_ref.shape[1]
                part = jnp.dot(a_ref[...], wbf_ref[w_slot, k0:k0 + kp, cs], preferred_element_type=f32)
                acc = part if acc is None else acc + part
                k0 += kp
            return acc

        for s in range(bn // sub):
            cs = slice(s * sub, (s + 1) * sub)
            d = lhs_dot(slot, cs)
            if nw == 2:
                d = d * jax.nn.sigmoid(d) * lhs_dot(slot + 1, cs)
            if act == "gelu":
                d = _gelu_tanh(d)
            if has_res:
                d = d + r_ref[:, cs]
            o_ref[:, cs] = d.astype(o_ref.dtype)

    if has_side:
        rs = s_ref.shape[0]
        slab = jnp.maximum(j - 1, 0) * nm + i * jnp.minimum(j, 1)
        full = slab < side_rows // rs

        @pl.when(full)
        def _():
            so_ref[...] = s_ref[...].astype(so_ref.dtype)

        @pl.when(jnp.logical_not(full))
        def _():
            row = slab * rs + lax.broadcasted_iota(jnp.int32, s_ref.shape, 0)
            so_ref[...] = jnp.where(row < side_rows, s_ref[...], 0.0).astype(so_ref.dtype)


def matmul_ws(a, ws, layer, n_cols, col0=0, res=None, act=None, out_dtype=f32, bm=1024, bn=1024,
              side=None, side_layer=0, w_transposed=False):
    a_parts = a if isinstance(a, (tuple, list)) else (a,)
    m = a_parts[0].shape[0]
    kdim = sum(p.shape[1] for p in a_parts)
    nw = len(ws)
    nm = m // bm
    ck = kdim // nm
    nt = pl.cdiv(n_cols, bn)
    ct0 = col0 // bn
    assert m % bm == 0 and kdim % nm == 0 and ck % 8 == 0 and col0 % bn == 0
    a_specs = [pl.BlockSpec((bm, p.shape[1]), lambda j, i: (i * jnp.minimum(j, 1), 0)) for p in a_parts]
    if w_transposed:
        w_spec = pl.BlockSpec((None, bn, ck), lambda j, i: (layer, ct0 + jnp.minimum(j, nt - 1), i))
    else:
        w_spec = pl.BlockSpec((None, ck, bn), lambda j, i: (layer, i, ct0 + jnp.minimum(j, nt - 1)))
    o_map = lambda j, i: (i * jnp.minimum(j, 1), jnp.maximum(j - 1, 0))
    in_specs = a_specs + [w_spec] * nw
    args = list(a_parts) + list(ws)
    if res is not None:
        in_specs.append(pl.BlockSpec((bm, bn), o_map))
        args.append(res)
    out_specs = [pl.BlockSpec((bm, bn), o_map)]
    out_shape = [jax.ShapeDtypeStruct((m, nt * bn), out_dtype)]
    side_rows = 0
    if side is not None:
        _, side_rows, side_cols = side.shape
        n_slabs = nt * nm
        rs = -(-side_rows // n_slabs)
        rs = -(-rs // 16) * 16
        last = side_rows // rs - (1 if side_rows % rs == 0 else 0)
        s_map = lambda j, i: (jnp.maximum(j - 1, 0) * nm + i * jnp.minimum(j, 1), 0)
        in_specs.append(pl.BlockSpec((None, rs, side_cols),
                                     lambda j, i: (side_layer, jnp.minimum(s_map(j, i)[0], last), 0)))
        args.append(side)
        out_specs.append(pl.BlockSpec((rs, side_cols), s_map))
        out_shape.append(jax.ShapeDtypeStruct((n_slabs * rs, side_cols), bf16))
    outs = pl.pallas_call(
        functools.partial(_mm_ws_kernel, na=len(a_parts), nw=nw, nt=nt, nm=nm, ck=ck, bn=bn,
                          n_valid=n_cols, act=act,
                          has_res=res is not None, has_side=side is not None, side_rows=side_rows,
                          w_transposed=w_transposed),
        grid=(nt + 1, nm),
        in_specs=in_specs,
        out_specs=out_specs,
        out_shape=out_shape,
        scratch_shapes=[pltpu.VMEM((2 * nw, kdim, bn), bf16)],
        compiler_params=_cparams(("arbitrary", "arbitrary")),
        name="matmul_ws",
    )(*args)
    return outs if side is not None else outs[0]


def _rope(x, c, s_lo, s_hi):
    half = ROPE_DIM // 2
    return x * c + pltpu.roll(x, HEAD_DIM - half, 1) * s_lo + pltpu.roll(x, half, 1) * s_hi


def _qkv_post_kernel(z_ref, c_ref, slo_ref, shi_ref, q_ref, kc_ref, vc_ref, ks_ref, kw_ref):
    c, s_lo, s_hi = c_ref[...], slo_ref[...], shi_ref[...]
    for h in range(NSA_HEADS):
        sl = slice(h * HEAD_DIM, (h + 1) * HEAD_DIM)
        q_ref[:, sl] = (_rope(z_ref[:, sl].astype(f32), c, s_lo, s_hi) * ATTN_SCALE).astype(q_ref.dtype)
    for idx, o_ref, roped in ((0, kc_ref, True), (1, vc_ref, False), (2, ks_ref, True), (4, kw_ref, True)):
        for g in range(NSA_KV_HEADS):
            col = NSA_WIDTH + idx * KV_WIDTH + g * HEAD_DIM
            x = z_ref[:, col:col + HEAD_DIM].astype(f32)
            if roped:
                x = _rope(x, c, s_lo, s_hi)
            o_ref[:, g * HEAD_DIM:(g + 1) * HEAD_DIM] = x.astype(o_ref.dtype)


def _rope_tables(t_len):
    half = ROPE_DIM // 2
    inv_freq = ROPE_THETA ** (-jnp.arange(half, dtype=f32) * 2.0 / ROPE_DIM)
    ang = jnp.arange(t_len).astype(f32)[:, None] * inv_freq[None, :]
    cos, sin = jnp.cos(ang), jnp.sin(ang)
    ones = jnp.ones((t_len, HEAD_DIM - ROPE_DIM), f32)
    zeros = jnp.zeros((t_len, HEAD_DIM - ROPE_DIM), f32)
    zh = jnp.zeros((t_len, half), f32)
    c = jnp.concatenate([cos, cos, ones], axis=1)
    s_lo = jnp.concatenate([-sin, zh, zeros], axis=1)
    s_hi = jnp.concatenate([zh, sin, zeros], axis=1)
    return c, s_lo, s_hi


def qkv_post(z, t_len, tm=256):
    m = z.shape[0]
    c, s_lo, s_hi = _rope_tables(t_len)
    nt = t_len // tm
    tab = pl.BlockSpec((tm, HEAD_DIM), lambda i: (i % nt, 0))
    kv = pl.BlockSpec((tm, KV_WIDTH), lambda i: (i, 0))
    shp = lambda dt: jax.ShapeDtypeStruct((m, KV_WIDTH), dt)
    return pl.pallas_call(
        _qkv_post_kernel,
        grid=(m // tm,),
        in_specs=[pl.BlockSpec((tm, z.shape[1]), lambda i: (i, 0)), tab, tab, tab],
        out_specs=[pl.BlockSpec((tm, NSA_WIDTH), lambda i: (i, 0)), kv, kv, kv, kv],
        out_shape=[jax.ShapeDtypeStruct((m, NSA_WIDTH), bf16), shp(f32), shp(f32), shp(bf16), shp(bf16)],
        compiler_params=_cparams(("parallel",)),
        name="qkv_post",
    )(z, c, s_lo, s_hi)


def _compress_kernel(a_ref, pe_ref, w1_ref, w2_ref, o_ref):
    n_chunks = a_ref.shape[1] // CMP_STRIDE
    top = bot = None
    for l in range(CMP_STRIDE):
        rows = a_ref[0, pl.ds(l, n_chunks, stride=CMP_STRIDE), :]
        t = jnp.dot((rows + pe_ref[l:l + 1, :]).astype(bf16), w1_ref[l].astype(bf16),
                    preferred_element_type=f32)
        u = jnp.dot((rows + pe_ref[CMP_STRIDE + l:CMP_STRIDE + l + 1, :]).astype(bf16),
                    w1_ref[CMP_STRIDE + l].astype(bf16), preferred_element_type=f32)
        top = t if top is None else top + t
        bot = u if bot is None else bot + u
    pre = top + pltpu.roll(bot, n_chunks - 1, 0)
    hid = _gelu_tanh(pre)
    o_ref[0, 0] = jnp.dot(hid.astype(bf16), w2_ref[...].astype(bf16),
                          preferred_element_type=f32).astype(o_ref.dtype)


def compress(a, pe, w1, w2):
    b, t_len, width = a.shape
    n_chunks = t_len // CMP_STRIDE
    return pl.pallas_call(
        _compress_kernel,
        grid=(b, width // HEAD_DIM),
        in_specs=[pl.BlockSpec((1, t_len, HEAD_DIM), lambda i, j: (i, 0, j)),
                  pl.BlockSpec(pe.shape, lambda i, j: (0, 0)),
                  pl.BlockSpec(w1.shape, lambda i, j: (0, 0, 0)),
                  pl.BlockSpec(w2.shape, lambda i, j: (0, 0))],
        out_specs=pl.BlockSpec((1, 1, n_chunks, HEAD_DIM), lambda i, j: (i, j, 0, 0)),
        out_shape=jax.ShapeDtypeStruct((b, width // HEAD_DIM, n_chunks, HEAD_DIM), bf16),
        compiler_params=_cparams(("parallel", "parallel")),
        name="compress",
    )(a, pe, w1, w2)


def _nsa_kernel(q_ref, kc_ref, vc_ref, ks_ref, vs_ref, kw_ref, vw_ref, g_ref, o_ref,
                imp_ref, score_ref, sel_ref, gate_ref):
    tq, ck = ATT_TQ, ATT_CK
    nl = NSA_HPG * tq
    grp = pl.program_id(1)
    qi = pl.program_id(2)
    t0 = qi * tq
    qt = jnp.concatenate([q_ref[0, :, h * HEAD_DIM:(h + 1) * HEAD_DIM].T for h in range(NSA_HPG)],
                         axis=1)
    n_cmp = kc_ref.shape[2]
    n_slc = sel_ref.shape[0]

    def pv(v, pr):
        return lax.dot_general(v, pr.astype(bf16), (((0,), (0,)), ((), ())), preferred_element_type=f32)

    def v_chunk(v_ref, c):
        return v_ref[0, pl.ds(pl.multiple_of(c * ck, ck), ck), :]

    kpos0 = lax.broadcasted_iota(jnp.int32, (ck, tq), 0)
    qpos = t0 + lax.broadcasted_iota(jnp.int32, (ck, tq), 1)
    c_last = (t0 + tq - 1) // ck

    def raw_scores(k_ref, c):
        k = k_ref[0, pl.ds(pl.multiple_of(c * ck, ck), ck), :]
        return jnp.dot(k, qt, preferred_element_type=f32)

    def masked(sc, mask):
        return jnp.where(jnp.concatenate([mask] * NSA_HPG, axis=1), sc, NEG)

    s = jnp.dot(kc_ref[0, 0], qt, preferred_element_type=f32)
    n_win = -(-tq // ck) + -(-(WINDOW - 1) // ck)
    c_win = [c_last - back for back in range(n_win)]
    s_win = [raw_scores(kw_ref, jnp.maximum(c, 0)) for c in c_win]

    def pair_scores(c):
        return raw_scores(ks_ref, c), raw_scores(ks_ref, c + 1)

    blk_end = lax.broadcasted_iota(jnp.int32, (n_cmp, nl), 0) * CMP_STRIDE + (CMP_BLOCK - 1)
    qpos4 = t0 + lax.broadcasted_iota(jnp.int32, (n_cmp, nl), 1) % tq
    valid = blk_end <= qpos4
    s = jnp.where(valid, s, NEG)
    mx = jnp.max(s, axis=0, keepdims=True)
    e = jnp.exp(s - mx)
    den = jnp.sum(e, axis=0, keepdims=True)
    p = e * jnp.where(mx > 0.5 * NEG, 1.0 / den, 0.0)
    o_cmp = pv(vc_ref[0, 0], p)
    imp = (p[:, 0:tq] + p[:, tq:2 * tq]) + p[:, 2 * tq:3 * tq] + p[:, 3 * tq:4 * tq]
    for lt in range(tq // LANE):
        imp_ref[lt] = imp[:, lt * LANE:(lt + 1) * LANE]

    for i, c in enumerate(c_win):
        kpos = kpos0 + c * ck
        s_win[i] = masked(s_win[i], (kpos <= qpos) & (kpos > qpos - WINDOW) & (kpos >= 0))
    m_w = functools.reduce(jnp.maximum, [jnp.max(sw, axis=0, keepdims=True) for sw in s_win])
    p_win = [jnp.exp(sw - m_w) for sw in s_win]
    l_w = functools.reduce(jnp.add, [jnp.sum(pw, axis=0, keepdims=True) for pw in p_win])
    acc_w = functools.reduce(jnp.add, [pv(v_chunk(vw_ref, jnp.maximum(c, 0)), pw)
                                       for c, pw in zip(c_win, p_win)])
    o_win = acc_w * (1.0 / l_w)

    r = SLC_BLOCK // CMP_STRIDE
    rows = [jnp.concatenate([imp_ref[lt, pl.ds(k, n_slc, stride=r), :] for lt in range(tq // LANE)],
                            axis=1) for k in range(r)]
    blk = lax.broadcasted_iota(jnp.int32, (n_slc, tq), 0)
    prev = jnp.where(blk == 0, 0.0, pltpu.roll(rows[r - 1], 1, 0))
    score = (rows[0] + rows[1] + rows[2]) + 0.5 * rows[r - 1] + 0.5 * prev
    cur = (t0 + lax.broadcasted_iota(jnp.int32, (n_slc, tq), 1)) // SLC_BLOCK
    score = jnp.where((blk == cur) | (blk == 0), -NEG, jnp.where(blk > cur, NEG, score))
    score_ref[...] = score
    sub = lax.broadcasted_iota(jnp.int32, (8, tq), 0)
    groups = [score[8 * gi:8 * gi + 8, :] for gi in range(n_slc // 8)]
    cnt = [jnp.zeros((8, tq), f32) for _ in groups]
    for jp in range(n_slc):
        row = jnp.broadcast_to(score_ref[pl.ds(jp, 1), :], (8, tq))
        for gi, sc_g in enumerate(groups):
            if jp < 8 * gi:
                beats = row >= sc_g
            elif jp >= 8 * gi + 8:
                beats = row > sc_g
            else:
                beats = (row > sc_g) | ((row == sc_g) & (sub > jp - 8 * gi))
            cnt[gi] = cnt[gi] + jnp.where(beats, 1.0, 0.0)
    sel_ref[...] = jnp.where(jnp.concatenate(cnt, axis=0) < min(SLC_TOPK, n_slc), 1.0, 0.0)

    init = (jnp.full((1, nl), NEG, f32), jnp.zeros((1, nl), f32), jnp.zeros((HEAD_DIM, nl), f32))

    def flash_update(c, sc, carry):
        m, l, acc = carry
        per_chunk = ck // SLC_BLOCK
        selrows = jnp.concatenate(
            [jnp.broadcast_to(sel_ref[pl.ds(c * per_chunk + i, 1), :], (SLC_BLOCK, tq))
             for i in range(per_chunk)], axis=0)
        sc = masked(sc, (selrows > 0.5) & (kpos0 + c * ck <= qpos))
        m_new = jnp.maximum(m, jnp.max(sc, axis=0, keepdims=True))
        alpha = jnp.exp(m - m_new)
        pr = jnp.exp(sc - m_new)
        l = alpha * l + jnp.sum(pr, axis=0, keepdims=True)
        acc = alpha * acc + pv(v_chunk(vs_ref, c), pr)
        return m_new, l, acc

    def pair_body(i, carry):
        s_a, s_b = pair_scores(2 * i)
        return flash_update(2 * i, s_a, carry[0]), flash_update(2 * i + 1, s_b, carry[1])

    (m_a, l_a, acc_a), (m_b, l_b, acc_b) = lax.fori_loop(0, (c_last + 2) // 2, pair_body, (init, init))
    m_s = jnp.maximum(m_a, m_b)
    w_a, w_b = jnp.exp(m_a - m_s), jnp.exp(m_b - m_s)
    r_s = 1.0 / (w_a * l_a + w_b * l_b)
    o_slc = (w_a * r_s) * acc_a + (w_b * r_s) * acc_b

    gate_ref[...] = jax.nn.sigmoid(g_ref[...].T)
    for h in range(NSA_HPG):
        sl = slice(h * tq, (h + 1) * tq)
        g_c, g_s, g_w = [gate_ref[pl.ds((grp * NSA_HPG + h) * 3 + br, 1), :] for br in range(3)]
        o_h = g_c * o_cmp[:, sl] + g_s * o_slc[:, sl] + g_w * o_win[:, sl]
        o_ref[0, :, h * HEAD_DIM:(h + 1) * HEAD_DIM] = o_h.T.astype(o_ref.dtype)


def nsa_attention(q, k_cmp, v_cmp, ks, kw, z, vs_col, vw_col, gates, t_len):
    b = q.shape[0] // t_len
    tq, ck = ATT_TQ, ATT_CK
    nq = t_len // tq
    n_cmp = k_cmp.shape[2]
    n_slc = t_len // SLC_BLOCK
    assert (t_len // ck) % 2 == 0
    assert vs_col % HEAD_DIM == 0 and vw_col % HEAD_DIM == 0
    seq = lambda a: a.reshape(b, t_len, a.shape[1])
    kv_full = pl.BlockSpec((1, t_len, HEAD_DIM), lambda i, g, j: (i, 0, g))
    vs_full = pl.BlockSpec((1, t_len, HEAD_DIM), lambda i, g, j: (i, 0, vs_col // HEAD_DIM + g))
    vw_full = pl.BlockSpec((1, t_len, HEAD_DIM), lambda i, g, j: (i, 0, vw_col // HEAD_DIM + g))
    cmp_full = pl.BlockSpec((1, 1, n_cmp, HEAD_DIM), lambda i, g, j: (i, g, 0, 0))
    return pl.pallas_call(
        _nsa_kernel,
        grid=(b, NSA_KV_HEADS, nq),
        in_specs=[pl.BlockSpec((1, tq, NSA_HPG * HEAD_DIM), lambda i, g, j: (i, j, g)),
                  cmp_full, cmp_full, kv_full, vs_full, kv_full, vw_full,
                  pl.BlockSpec((tq, LANE), lambda i, g, j: (i * nq + j, 0))],
        out_specs=pl.BlockSpec((1, tq, NSA_HPG * HEAD_DIM), lambda i, g, j: (i, j, g)),
        out_shape=jax.ShapeDtypeStruct((b, t_len, NSA_WIDTH), bf16),
        scratch_shapes=[pltpu.VMEM((tq // LANE, n_cmp, LANE), f32), pltpu.VMEM((n_slc, tq), f32),
                        pltpu.VMEM((n_slc, tq), f32), pltpu.VMEM((LANE, tq), f32)],
        compiler_params=_cparams(("parallel", "parallel", "arbitrary")),
        name="nsa_attention",
    )(seq(q), k_cmp, v_cmp, seq(ks), seq(z), seq(kw), seq(z), gates)


def _pool_kernel(x_ref, halo_ref, w_ref, scale_ref, o_ref, *, t_len):
    tm = x_ref.shape[0]
    i = pl.program_id(0)
    t_start = (i * tm) % t_len
    halo = jnp.where(t_start == 0, 0.0, halo_ref[...])
    tpos = t_start + lax.broadcasted_iota(jnp.int32, (tm, 1), 0)
    kw = w_ref.shape[1]
    for gi, w_len in enumerate(POOL_WINDOWS):
        sl = slice(gi * POOL_GROUP, (gi + 1) * POOL_GROUP)
        xs = slice(gi * POOL_GROUP, gi * POOL_GROUP + kw)
        x = x_ref[:, xs]
        acc = jnp.concatenate([halo[:, xs], x], axis=0)
        first = -POOL_HALO
        span = 1
        while span < w_len:
            acc = acc[span:, :] + acc[:-span, :]
            first += span
            span *= 2
        wsum = acc[-first:-first + tm, :]
        count = jnp.minimum(tpos + 1, w_len).astype(f32)
        y = (wsum / count - x).astype(bf16)
        y = jnp.dot(y, w_ref[gi].astype(bf16), preferred_element_type=f32)
        o_ref[:, sl] = (y * scale_ref[:, sl]).astype(o_ref.dtype)


def pool_mixer(x, w, scale, t_len, lane0=0, tm=256):
    m = x.shape[0]
    groups, cg, _ = w.shape
    c = groups * cg
    kw = -(-(lane0 + cg) // LANE) * LANE
    xc = (groups - 1) * cg + kw
    assert lane0 < LANE and xc <= x.shape[1]
    w_shift = jnp.pad(w, ((0, 0), (lane0, kw - lane0 - cg), (0, 0)))
    hb = tm // POOL_HALO
    return pl.pallas_call(
        functools.partial(_pool_kernel, t_len=t_len),
        grid=(m // tm,),
        in_specs=[pl.BlockSpec((tm, xc), lambda i: (i, 0)),
                  pl.BlockSpec((POOL_HALO, xc), lambda i: (jnp.maximum(i * hb - 1, 0), 0)),
                  pl.BlockSpec(w_shift.shape, lambda i: (0, 0, 0)),
                  pl.BlockSpec((1, c), lambda i: (0, 0))],
        out_specs=pl.BlockSpec((tm, c), lambda i: (i, 0)),
        out_shape=jax.ShapeDtypeStruct((m, c), bf16),
        compiler_params=_cparams(("parallel",)),
        name="pool_mixer",
    )(x, x, w_shift, scale.reshape(1, c))


def _gmlp_kernel(u_ref, v_ref, lg_ref, lb_ref, ws_ref, bs_ref, o_ref, vn_ref):
    tc = v_ref.shape[0]
    vf = v_ref[...].astype(f32)
    mu = jnp.mean(vf, axis=-1, keepdims=True)
    d = vf - mu
    var = jnp.mean(d * d, axis=-1, keepdims=True)
    vn_ref[...] = (d * lax.rsqrt(var + EPS) * lg_ref[...] + lb_ref[...]).astype(bf16)
    gd = v_ref.shape[1] // GMLP_GROUPS
    tri = lax.broadcasted_iota(jnp.int32, (tc, tc), 0) >= lax.broadcasted_iota(jnp.int32, (tc, tc), 1)
    for g in range(GMLP_GROUPS):
        sl = slice(g * gd, (g + 1) * gd)
        w = jnp.where(tri, ws_ref[g], 0.0).astype(bf16)
        bias = jnp.concatenate([bs_ref[g]] * (gd // LANE), axis=1)
        s = jnp.dot(w, vn_ref[:, sl], preferred_element_type=f32) + bias
        o_ref[:, sl] = u_ref[:, sl] * s.astype(o_ref.dtype)


def gmlp_gate(z, ln_g, ln_b, ws, bs):
    m, two_w = z.shape
    width = two_w // 2
    tc = GMLP_CHUNK
    row = pl.BlockSpec((1, width), lambda i: (0, 0))
    return pl.pallas_call(
        _gmlp_kernel,
        grid=(m // tc,),
        in_specs=[pl.BlockSpec((tc, width), lambda i: (i, 0)),
                  pl.BlockSpec((tc, width), lambda i: (i, 1)),
                  row, row,
                  pl.BlockSpec(ws.shape, lambda i: (0, 0, 0)),
                  pl.BlockSpec((GMLP_GROUPS, tc, LANE), lambda i: (0, 0, 0))],
        out_specs=pl.BlockSpec((tc, width), lambda i: (i, 0)),
        out_shape=jax.ShapeDtypeStruct((m, width), bf16),
        scratch_shapes=[pltpu.VMEM((tc, width), bf16)],
        compiler_params=_cparams(("parallel",)),
        name="gmlp_gate",
    )(z, z, ln_g.reshape(1, width), ln_b.reshape(1, width), ws,
      jnp.broadcast_to(bs[:, :, None], (GMLP_GROUPS, tc, LANE)))


def _ffn(x, norm_g, wg, wu, wd, layer):
    h = rms_norm(x, norm_g, bf16)
    gu, wd_bf = matmul_ws(h, (wg, wu), layer, wg.shape[2], out_dtype=bf16, bn=FFN_PAD,
                          side=wd, side_layer=layer)
    return matmul_res(gu, wd_bf, x)


def _even_layer(x, b, t_len, norm_g, w_in, j, pe_k, pe_v, w1_k, w2_k, w1_v, w2_v, pool_w, pool_scale, w_out):
    m = x.shape[0]
    h = rms_norm(x, norm_g, bf16)
    qkv_w = NSA_WIDTH + 6 * KV_WIDTH
    w_in_t = jnp.swapaxes(w_in, 1, 2)
    z = matmul_ws(h, (w_in_t,), j, qkv_w, out_dtype=bf16, w_transposed=True)
    zgp = matmul_ws(h, (w_in_t,), j, w_in.shape[2] - qkv_w, col0=qkv_w, bn=512, w_transposed=True)

    q, kc, vc, ks, kw = qkv_post(z, t_len)
    seq = lambda a: a.reshape(b, t_len, a.shape[1])
    k_cmp = compress(seq(kc), pe_k, w1_k, w2_k)
    v_cmp = compress(seq(vc), pe_v, w1_v, w2_v)
    o = nsa_attention(q, k_cmp, v_cmp, ks, kw, z, NSA_WIDTH + 3 * KV_WIDTH, NSA_WIDTH + 5 * KV_WIDTH,
                      zgp, t_len)
    y_pool = pool_mixer(zgp, pool_w, pool_scale, t_len, lane0=N_GATES)
    return matmul_ws((o.reshape(m, NSA_WIDTH), y_pool), (w_out,), j, w_out.shape[2], res=x, bn=512)


def _odd_layer(x, norm_g, w_in, j, ln_g, ln_b, ws, bs, w_out):
    h = rms_norm(x, norm_g, bf16)
    z, w_out_bf = matmul_ws(h, (w_in,), j, w_in.shape[2], act="gelu", out_dtype=bf16,
                            side=w_out, side_layer=j)
    y = gmlp_gate(z, ln_g, ln_b, ws, bs)
    return matmul_res(y, w_out_bf, x)


def kernel(x, norm_mix_even, w_in_even, cmp_pe_k, cmp_pe_v, cmp_w1_k, cmp_w2_k, cmp_w1_v, cmp_w2_v, pool_w, pool_scale, w_out_even, norm_mix_odd, w_in_odd, gmlp_ln_g, gmlp_ln_b, gmlp_ws, gmlp_bs, w_out_odd, norm_ffn, w_ffn_gate, w_ffn_up, w_ffn_down, norm_final):
    b, t_len, d = x.shape
    depth = norm_ffn.shape[0]
    xf = x.reshape(b * t_len, d)
    for layer in range(depth):
        j = layer // 2
        if layer % 2 == 0:
            xf = _even_layer(xf, b, t_len, norm_mix_even[j], w_in_even, j, cmp_pe_k[j], cmp_pe_v[j],
                             cmp_w1_k[j], cmp_w2_k[j], cmp_w1_v[j], cmp_w2_v[j], pool_w[j],
                             pool_scale[j], w_out_even)
        else:
            xf = _odd_layer(xf, norm_mix_odd[j], w_in_odd, j, gmlp_ln_g[j], gmlp_ln_b[j], gmlp_ws[j],
                            gmlp_bs[j], w_out_odd)
        xf = _ffn(xf, norm_ffn[layer], w_ffn_gate, w_ffn_up, w_ffn_down, layer)
    return rms_norm(xf, norm_final, x.dtype).reshape(b, t_len, d)
```
